```python
import jax, jax.numpy as jnp
from jax import lax
import numpy as np

D_MODEL = 1024
BATCH = 8
SEQ = 2048
DEPTH = 1
DEC_BATCH = 16
DEC_SEQ = 2048
PAST_LEN = 128

HEAD_DIM = 64
N_HEADS_A = 8
N_HEADS_B = 8
WIDTH_A = N_HEADS_A * HEAD_DIM
WIDTH_B = N_HEADS_B * HEAD_DIM
DILATED_PATTERNS = ((128, 1), (512, 4), (2048, 16))
ROPE_THETA = 500000.0
ROPE_DIM = HEAD_DIM // 4
GRID_W = 64
NA_ROWS_MAX = 8
NA_COLS = 16
NA_COL_BLOCK = NA_COLS
NA_COL_SPAN = 2 * NA_COLS
RPB_ROWS = 2 * NA_ROWS_MAX - 1
RPB_COLS = 2 * NA_COLS - 1
D_FF = 2816
CONV_W = 3
EPS = 1e-6
NEG_INF = -1e30
IN_SPLITS = (WIDTH_A, 2 * WIDTH_A, 3 * WIDTH_A, 3 * WIDTH_A + WIDTH_B,
             3 * WIDTH_A + 2 * WIDTH_B, 3 * WIDTH_A + 3 * WIDTH_B, 3 * WIDTH_A + 3 * WIDTH_B + D_MODEL)
IN_COLS = 3 * WIDTH_A + 3 * WIDTH_B + 2 * D_MODEL

kernel_name = "hybrid_dilated_neighbourhood_encoder"


def _rms_norm(x, g):
    xf = x.astype(jnp.float32)
    y = xf * lax.rsqrt(jnp.mean(xf * xf, axis=-1, keepdims=True) + EPS)
    return (y * g.astype(jnp.float32)).astype(x.dtype)


def _heads(t, n_heads):
    return t.reshape(t.shape[0], t.shape[1], n_heads, HEAD_DIM)


def _partial_rope(x, pos):
    half = ROPE_DIM // 2
    inv = jnp.power(jnp.float32(ROPE_THETA), -jnp.arange(half, dtype=jnp.float32) / half)
    ang = pos.astype(jnp.float32)[:, None] * inv[None, :]
    cos = jnp.cos(ang)[None, :, None, :]
    sin = jnp.sin(ang)[None, :, None, :]
    xr = x[..., :ROPE_DIM].astype(jnp.float32)
    x1, x2 = xr[..., :half], xr[..., half:]
    rot = jnp.concatenate([x1 * cos - x2 * sin, x1 * sin + x2 * cos], axis=-1).astype(x.dtype)
    return jnp.concatenate([rot, x[..., ROPE_DIM:]], axis=-1)


def _banded_window_attention(q, k, v, half):
    n, L, H, dh = q.shape
    blk = half
    nb = -(-L // blk)
    pad = nb * blk - L
    qb = jnp.pad(q, ((0, 0), (0, pad), (0, 0), (0, 0))).reshape(n, nb, blk, H, dh)

    def band(t):
        tb = jnp.pad(t, ((0, 0), (blk, pad + blk), (0, 0), (0, 0))).reshape(n, nb + 2, blk, H, dh)
        return jnp.concatenate([tb[:, :-2], tb[:, 1:-1], tb[:, 2:]], axis=2)

    kw, vw = band(k), band(v)
    s = jnp.einsum("nbqhd,nbkhd->nbhqk", qb, kw, preferred_element_type=jnp.float32) * (HEAD_DIM ** -0.5)
    qi = jnp.arange(nb)[:, None] * blk + jnp.arange(blk)[None, :]
    kj = jnp.arange(nb)[:, None] * blk - blk + jnp.arange(3 * blk)[None, :]
    valid = ((jnp.abs(kj[:, None, :] - qi[:, :, None]) <= half)
             & (kj[:, None, :] >= 0) & (kj[:, None, :] < L))
    s = jnp.where(valid[None, :, None], s, NEG_INF)
    m = jnp.max(s, axis=-1, keepdims=True)
    p = jnp.exp(s - m)
    den = jnp.sum(p, axis=-1, keepdims=True)
    o = jnp.einsum("nbhqk,nbkhd->nbqhd", (p / den).astype(v.dtype), vw)
    lse = (m + jnp.log(den))[..., 0]
    o = o.reshape(n, nb * blk, H, dh)[:, :L]
    lse = lse.transpose(0, 1, 3, 2).reshape(n, nb * blk, H)[:, :L]
    return o, lse


def _dilated_mixture_attention(q, k, v):
    B, S, H, dh = q.shape
    outs, lses = [], []
    for window, dil in DILATED_PATTERNS:
        half = window // (2 * dil)
        L = S // dil

        def to_res(t):
            return t.reshape(B, L, dil, H, dh).transpose(0, 2, 1, 3, 4).reshape(B * dil, L, H, dh)

        o, lse = _banded_window_attention(to_res(q), to_res(k), to_res(v), half)
        outs.append(o.reshape(B, dil, L, H, dh).transpose(0, 2, 1, 3, 4).reshape(B, S, H, dh))
        lses.append(lse.reshape(B, dil, L, H).transpose(0, 2, 1, 3).reshape(B, S, H))
    w = jax.nn.softmax(jnp.stack(lses), axis=0)
    out = jnp.einsum("pbsh,pbshd->bshd", w, jnp.stack(outs).astype(jnp.float32))
    return out.astype(q.dtype)


def _neighbourhood_attention(q, k, v, rpb):
    B, S, H, dh = q.shape
    rows = S // GRID_W
    kh = min(NA_ROWS_MAX, rows)
    ncb = GRID_W // NA_COL_BLOCK
    col_q = jnp.arange(GRID_W).reshape(ncb, NA_COL_BLOCK)
    cs = jnp.clip(col_q - NA_COLS // 2, 0, GRID_W - NA_COLS)
    span0 = jnp.clip(jnp.arange(ncb) * NA_COL_BLOCK - NA_COLS // 2, 0, GRID_W - NA_COL_SPAN)
    col_k = span0[:, None] + jnp.arange(NA_COL_SPAN)[None, :]
    col_ok = (col_k[:, None, :] >= cs[:, :, None]) & (col_k[:, None, :] < cs[:, :, None] + NA_COLS)
    col_rel = jnp.clip(col_k[:, None, :] - col_q[:, :, None] + NA_COLS - 1, 0, RPB_COLS - 1)
    qg = q.reshape(B, rows, ncb, NA_COL_BLOCK, H, dh)
    kg = k.reshape(B, rows, GRID_W, H, dh)[:, :, col_k]
    vg = v.reshape(B, rows, GRID_W, H, dh)[:, :, col_k]
    scale = HEAD_DIM ** -0.5

    def one_row(r):
        rs = jnp.clip(r - kh // 2, 0, rows - kh)
        kr = lax.dynamic_slice_in_dim(kg, rs, kh, axis=1)
        vr = lax.dynamic_slice_in_dim(vg, rs, kh, axis=1)
        qr = lax.dynamic_index_in_dim(qg, r, axis=1, keepdims=False)
        row_rel = rs + jnp.arange(kh) - r + NA_ROWS_MAX - 1
        bias = rpb[:, row_rel[:, None, None, None], col_rel[None]]
        bias = bias.transpose(0, 2, 3, 1, 4).astype(jnp.float32)
        s = jnp.einsum("bnqhd,binkhd->bhnqik", qr, kr, preferred_element_type=jnp.float32) * scale + bias[None]
        s = jnp.where(col_ok[None, None, :, :, None, :], s, NEG_INF)
        p = jax.nn.softmax(s.reshape(B, H, ncb, NA_COL_BLOCK, kh * NA_COL_SPAN), axis=-1)
        p = p.reshape(s.shape).astype(v.dtype)
        return jnp.einsum("bhnqik,binkhd->bnqhd", p, vr)

    out = lax.map(one_row, jnp.arange(rows))
    return jnp.moveaxis(out, 0, 1).reshape(B, S, H, dh)


def _token_mixer(h, w_in, rpb, w_branch_a, w_branch_b, w_out):
    B, S, _ = h.shape
    proj = jnp.einsum("bsd,de->bse", h, w_in)
    qa, ka, va, qb, kb, vb, ga, gb = jnp.split(proj, IN_SPLITS, axis=-1)
    pos = jnp.arange(S)
    qa = _partial_rope(_heads(qa, N_HEADS_A), pos)
    ka = _partial_rope(_heads(ka, N_HEADS_A), pos)
    ya = _dilated_mixture_attention(qa, ka, _heads(va, N_HEADS_A)).reshape(B, S, WIDTH_A)
    yb = _neighbourhood_attention(_heads(qb, N_HEADS_B), _heads(kb, N_HEADS_B),
                                  _heads(vb, N_HEADS_B), rpb).reshape(B, S, WIDTH_B)
    ya = jnp.einsum("bse,ed->bsd", ya, w_branch_a)
    yb = jnp.einsum("bse,ed->bsd", yb, w_branch_b)
    merged = jax.nn.sigmoid(ga) * ya + jax.nn.sigmoid(gb) * yb
    return jnp.einsum("bsd,de->bse", merged, w_out)


def _conv_ffn(h, w_up, conv_w, conv_b, w_down):
    S = h.shape[1]
    u = jnp.einsum("bsd,df->bsf", h, w_up)
    pad = CONV_W // 2
    up = jnp.pad(u, ((0, 0), (pad, pad), (0, 0)))
    u = sum(conv_w[t] * up[:, t:t + S] for t in range(CONV_W)) + conv_b
    val, gate = jnp.split(u, 2, axis=-1)
    return jnp.einsum("bsf,fd->bsd", jax.nn.gelu(gate, approximate=True) * val, w_down)


def _trunk(x, c, w_ada, b_ada, g_mix_pre, g_mix_post, g_ffn_pre, g_ffn_post, w_in, rpb,
           w_branch_a, w_branch_b, w_out, w_up, conv_w, conv_b, w_down):
    for l in range(DEPTH):
        mod = jnp.einsum("bd,de->be", jax.nn.silu(c), w_ada[l]) + b_ada[l]
        sh1, sc1, gt1, sh2, sc2, gt2 = jnp.split(mod[:, None, :], 6, axis=-1)
        h = _rms_norm(x, g_mix_pre[l]) * (1 + sc1) + sh1
        x = x + gt1 * _rms_norm(_token_mixer(h, w_in[l], rpb[l], w_branch_a[l], w_branch_b[l], w_out[l]), g_mix_post[l])
        h = _rms_norm(x, g_ffn_pre[l]) * (1 + sc2) + sh2
        x = x + gt2 * _rms_norm(_conv_ffn(h, w_up[l], conv_w[l], conv_b[l], w_down[l]), g_ffn_post[l])
    return x


def setup_inputs(seed: int = 0) -> dict:
    key = jax.random.key(seed)
    ks = jax.random.split(key, 20)
    D, L = D_MODEL, DEPTH

    def nrm(k, shape, scale):
        return jax.random.normal(k, shape, jnp.float32) * scale

    return {
        "x_prompt": nrm(ks[0], (BATCH, SEQ, D), 1.0),
        "x_sample": nrm(ks[1], (DEC_BATCH, DEC_SEQ, D), 1.0),
        "c_prompt": nrm(ks[2], (BATCH, D), 1.0),
        "c_sample": nrm(ks[3], (DEC_BATCH, D), 1.0),
        "w_ada": nrm(ks[4], (L, D, 6 * D), 0.5 * D ** -0.5),
        "b_ada": nrm(ks[5], (L, 6 * D), 0.02),
        "g_mix_pre": 1.0 + nrm(ks[6], (L, D), 0.05),
        "g_mix_post": 1.0 + nrm(ks[7], (L, D), 0.05),
        "g_ffn_pre": 1.0 + nrm(ks[8], (L, D), 0.05),
        "g_ffn_post": 1.0 + nrm(ks[9], (L, D), 0.05),
        "w_in": nrm(ks[10], (L, D, IN_COLS), D ** -0.5),
        "rpb": nrm(ks[11], (L, N_HEADS_B, RPB_ROWS, RPB_COLS), 0.1),
        "w_branch_a": nrm(ks[12], (L, WIDTH_A, D), WIDTH_A ** -0.5),
        "w_branch_b": nrm(ks[13], (L, WIDTH_B, D), WIDTH_B ** -0.5),
        "w_out": nrm(ks[14], (L, D, D), D ** -0.5),
        "w_up": nrm(ks[15], (L, D, 2 * D_FF), D ** -0.5),
        "conv_w": nrm(ks[16], (L, CONV_W, 2 * D_FF), CONV_W ** -0.5),
        "conv_b": nrm(ks[17], (L, 2 * D_FF), 0.02),
        "w_down": nrm(ks[18], (L, D_FF, D), D_FF ** -0.5),
    }


def reference(x_prompt, x_sample, c_prompt, c_sample, w_ada, b_ada, g_mix_pre, g_mix_post,
              g_ffn_pre, g_ffn_post, w_in, rpb, w_branch_a, w_branch_b, w_out, w_up, conv_w,
              conv_b, w_down):
    y_prompt = _trunk(x_prompt, c_prompt, w_ada, b_ada, g_mix_pre, g_mix_post, g_ffn_pre, g_ffn_post,
                      w_in, rpb, w_branch_a, w_branch_b, w_out, w_up, conv_w, conv_b, w_down)
    y_sample = _trunk(x_sample, c_sample, w_ada, b_ada, g_mix_pre, g_mix_post, g_ffn_pre, g_ffn_post,
                      w_in, rpb, w_branch_a, w_branch_b, w_out, w_up, conv_w, conv_b, w_down)
    return (y_prompt, y_sample)
```

```python
import functools

import numpy as np
import jax
import jax.numpy as jnp
from jax import lax
from jax.experimental import pallas as pl
from jax.experimental.pallas import tpu as pltpu

D_MODEL = 1024
HEAD_DIM = 64
N_HEADS = 8
WIDTH = N_HEADS * HEAD_DIM
DILATIONS = (1, 4, 16)
BAND_HALF = 64
ROPE_THETA = 500000.0
ROPE_DIM = HEAD_DIM // 4
GRID_W = 64
NA_ROWS = 8
NA_COLS = 16
D_FF = 2816
EPS = 1e-6
NEG_INF = -1e30

LANES = 128
Q_BLOCK = 128
VMEM_LIMIT = 56 * 1024 * 1024

F32 = jnp.float32
BF16 = jnp.bfloat16


def _params(*sem):
    return pltpu.CompilerParams(dimension_semantics=sem, vmem_limit_bytes=VMEM_LIMIT)


def _const_spec(shape):
    nd = len(shape)
    return pl.BlockSpec(shape, lambda *_: (0,) * nd)


def _rms(x, g):
    return x * lax.rsqrt(jnp.mean(x * x, axis=-1, keepdims=True) + EPS) * g


def _mod_kernel(c_ref, w_ref, b_ref, o_ref):
    c = c_ref[...]
    s = c * jax.nn.sigmoid(c)
    w = w_ref[...]
    s_hi = s.astype(BF16)
    s_lo = (s - s_hi.astype(F32)).astype(BF16)
    w_hi = w.astype(BF16)
    w_lo = (w - w_hi.astype(F32)).astype(BF16)
    acc = jnp.dot(s_hi, w_hi, preferred_element_type=F32)
    acc += jnp.dot(s_hi, w_lo, preferred_element_type=F32)
    acc += jnp.dot(s_lo, w_hi, preferred_element_type=F32)
    o_ref[...] = acc + b_ref[...]


def _modulation(c, w_ada, b_ada):
    b, d = c.shape
    n = w_ada.shape[1]
    return pl.pallas_call(
        _mod_kernel,
        grid=(n // d,),
        in_specs=[_const_spec((b, d)),
                  pl.BlockSpec((d, d), lambda j: (0, j)),
                  pl.BlockSpec((1, d), lambda j: (0, j))],
        out_specs=pl.BlockSpec((b, d), lambda j: (0, j)),
        out_shape=jax.ShapeDtypeStruct((b, n), F32),
        compiler_params=_params("arbitrary"),
        name="mod",
    )(c, w_ada, b_ada.reshape(1, n))


def _inproj_kernel(x_ref, mod_ref, g_ref, w_ref, cos_ref, sa_ref, sb_ref,
                   qa_ref, ka_ref, va_ref, qb_ref, kb_ref, vb_ref, ga_ref, gb_ref):
    h = _rms(x_ref[...], g_ref[...]) * (1.0 + mod_ref[1:2, :]) + mod_ref[0:1, :]
    hb = h.astype(BF16)

    def proj(c0, width):
        return jnp.dot(hb, w_ref[:, c0:c0 + width], preferred_element_type=F32)

    cos, sa, sb = cos_ref[...], sa_ref[...], sb_ref[...]

    def rope(t):
        outs = []
        for g in range(t.shape[1] // LANES):
            tg = t[:, g * LANES:(g + 1) * LANES]
            outs.append(tg * cos + pltpu.roll(tg, ROPE_DIM // 2, 1) * sa
                        + pltpu.roll(tg, LANES - ROPE_DIM // 2, 1) * sb)
        return jnp.concatenate(outs, axis=1)

    scale = HEAD_DIM ** -0.5
    qa_ref[...] = (rope(proj(0, WIDTH)) * scale).astype(BF16)
    ka_ref[...] = rope(proj(WIDTH, WIDTH)).astype(BF16)
    va_ref[...] = proj(2 * WIDTH, WIDTH).astype(BF16)
    qb_ref[...] = (proj(3 * WIDTH, WIDTH) * scale).astype(BF16)
    kb_ref[...] = proj(4 * WIDTH, WIDTH).astype(BF16)
    vb_ref[...] = proj(5 * WIDTH, WIDTH).astype(BF16)
    ga_ref[...] = jax.nn.sigmoid(proj(6 * WIDTH, D_MODEL)).astype(BF16)
    gb_ref[...] = jax.nn.sigmoid(proj(6 * WIDTH + D_MODEL, D_MODEL)).astype(BF16)


def _rope_tables(seq):
    half = ROPE_DIM // 2
    inv = jnp.power(jnp.float32(ROPE_THETA), -jnp.arange(half, dtype=F32) / half)
    ang = jnp.arange(seq).astype(F32)[:, None] * inv[None, :]
    cos, sin = jnp.cos(ang), jnp.sin(ang)
    ones = jnp.ones((seq, HEAD_DIM - ROPE_DIM), F32)
    zeros = jnp.zeros((seq, HEAD_DIM - ROPE_DIM), F32)
    zh = jnp.zeros((seq, half), F32)
    per_head = lambda *parts: jnp.tile(jnp.concatenate(parts, axis=1), (1, LANES // HEAD_DIM))
    return (per_head(cos, cos, ones),
            per_head(zh, sin, zeros),
            per_head(-sin, zh, zeros))


def _inproj(x2, mod3, g_pre, w_in, tables, seq, tm):
    t, d = x2.shape
    tps = seq // tm
    row = lambda w: pl.BlockSpec((tm, w), lambda i: (i, 0))
    tab = pl.BlockSpec((tm, LANES), lambda i: (i % tps, 0))
    outs = [jax.ShapeDtypeStruct((t, WIDTH), BF16)] * 6 + [jax.ShapeDtypeStruct((t, d), BF16)] * 2
    return pl.pallas_call(
        _inproj_kernel,
        grid=(t // tm,),
        in_specs=[row(d),
                  pl.BlockSpec((None, 6, d), lambda i: (i // tps, 0, 0)),
                  _const_spec((1, d)),
                  _const_spec(w_in.shape),
                  tab, tab, tab],
        out_specs=[row(WIDTH)] * 6 + [row(d)] * 2,
        out_shape=outs,
        compiler_params=_params("parallel"),
        name="inproj",
    )(x2, mod3, g_pre, w_in, *tables)


def _pair_attention(q2, kwin, vwin, bias_of_head, lo_mask):
    n = kwin.shape[0]
    vaug = jnp.concatenate([vwin, jnp.ones((n, LANES), BF16)], axis=1)
    halves = []
    for hh in range(2):
        keep = lo_mask if hh == 0 else jnp.logical_not(lo_mask)
        qh = jnp.where(keep, q2, jnp.zeros_like(q2))
        s = lax.dot_general(qh, kwin, (((1,), (1,)), ((), ())), preferred_element_type=F32)
        s = s + bias_of_head(hh)
        m = jnp.max(s, axis=1, keepdims=True)
        p = jnp.exp(s - m).astype(BF16)
        r = jnp.dot(p, vaug, preferred_element_type=F32)
        den = r[:, LANES:]
        halves.append((r[:, :LANES] / den, m + jnp.log(den)))
    o = jnp.where(lo_mask, halves[0][0], halves[1][0])
    lse = jnp.where(lo_mask, halves[0][1], halves[1][1])
    return o, lse


def _lo_mask(rows):
    return lax.broadcasted_iota(jnp.int32, (rows, LANES), 1) < HEAD_DIM


def _band_kernel(bias_ref, q_ref, k_ref, v_ref, o_ref, lse_ref, *, length, groups):
    nblk = length // Q_BLOCK
    kw = bias_ref.shape[2]
    lo = _lo_mask(Q_BLOCK)

    def block(blk, g):
        cols = slice(g * LANES, (g + 1) * LANES)
        if nblk == 1:
            i0, ks, kind = 0, 0, 0
        else:
            i0 = pl.multiple_of(blk * Q_BLOCK, Q_BLOCK)
            ks = pl.multiple_of(jnp.clip(i0 - BAND_HALF, 0, length - kw), BAND_HALF)
            kind = jnp.where(blk == 0, 0, jnp.where(blk == nblk - 1, 2, 1))
        bias = bias_ref[kind]
        o, lse = _pair_attention(q_ref[pl.ds(i0, Q_BLOCK), cols],
                                 k_ref[pl.ds(ks, kw), cols],
                                 v_ref[pl.ds(ks, kw), cols],
                                 lambda hh: bias, lo)
        o_ref[pl.ds(i0, Q_BLOCK), cols] = o.astype(BF16)
        lse_ref[pl.ds(i0, Q_BLOCK), cols] = lse

    for g in range(groups):
        if nblk == 1:
            block(0, g)
        else:
            def body(blk, carry, g=g):
                block(blk, g)
                return carry
            lax.fori_loop(0, nblk, body, 0)


def _band_bias(length):
    kw = min(2 * Q_BLOCK, length)
    qi = np.arange(Q_BLOCK)[:, None]
    kj = np.arange(kw)[None, :]
    offsets = (0,) if length == Q_BLOCK else (0, -BAND_HALF, Q_BLOCK - kw)
    tiles = [np.where(np.abs(kj + off - qi) <= BAND_HALF, 0.0, NEG_INF) for off in offsets]
    return jnp.asarray(np.stack(tiles), F32)


def _band_attention(q, k, v, batch, seq, dil):
    length = seq // dil
    width = dil * WIDTH
    bw = min(width, 16 * LANES)
    view = lambda a: a.reshape(batch, length, width)
    spec = pl.BlockSpec((None, length, bw), lambda b, j: (b, 0, j))
    bias = _band_bias(length)
    o, lse = pl.pallas_call(
        functools.partial(_band_kernel, length=length, groups=bw // LANES),
        grid=(batch, width // bw),
        in_specs=[_const_spec(bias.shape), spec, spec, spec],
        out_specs=[spec, spec],
        out_shape=[jax.ShapeDtypeStruct((batch, length, width), BF16),
                   jax.ShapeDtypeStruct((batch, length, width), F32)],
        compiler_params=_params("parallel", "parallel"),
        name=f"band{dil}",
    )(bias, view(q), view(k), view(v))
    return o.reshape(batch * seq, WIDTH), lse.reshape(batch * seq, WIDTH)


def _nbr_kernel(bias_ref, q_ref, k_ref, v_ref, o_ref, *, rows):
    lo = _lo_mask(GRID_W)
    nkeys = NA_ROWS * GRID_W

    def row(r, carry):
        rs = jnp.clip(r - NA_ROWS // 2, 0, rows - NA_ROWS)
        var = r - rs
        q0 = pl.multiple_of(r * GRID_W, GRID_W)
        k0 = pl.multiple_of(rs * GRID_W, GRID_W)
        for p in range(WIDTH // LANES):
            cols = slice(p * LANES, (p + 1) * LANES)
            o, _ = _pair_attention(q_ref[pl.ds(q0, GRID_W), cols],
                                   k_ref[pl.ds(k0, nkeys), cols],
                                   v_ref[pl.ds(k0, nkeys), cols],
                                   lambda hh, p=p: bias_ref[(2 * p + hh) * NA_ROWS + var], lo)
            o_ref[pl.ds(q0, GRID_W), cols] = o.astype(BF16)
        return carry

    lax.fori_loop(0, rows, row, 0)


def _nbr_bias(rpb):
    var = np.arange(NA_ROWS)[:, None]
    row_rel = np.arange(NA_ROWS)[None, :] - var + NA_ROWS - 1
    c = np.arange(GRID_W)[:, None]
    ck = np.arange(GRID_W)[None, :]
    cs = np.clip(c - NA_COLS // 2, 0, GRID_W - NA_COLS)
    ok = (ck >= cs) & (ck < cs + NA_COLS)
    col_rel = np.clip(ck - c + NA_COLS - 1, 0, 2 * NA_COLS - 2)
    b = rpb[:, row_rel[:, :, None, None], col_rel[None, None]]
    b = jnp.where(ok[None, None, None], b.astype(F32), NEG_INF)
    b = b.transpose(0, 1, 3, 2, 4)
    return b.reshape(N_HEADS * NA_ROWS, GRID_W, NA_ROWS * GRID_W)


def _nbr_attention(q, k, v, bias, batch, seq):
    view = lambda a: a.reshape(batch, seq, WIDTH)
    spec = pl.BlockSpec((None, seq, WIDTH), lambda b: (b, 0, 0))
    o = pl.pallas_call(
        functools.partial(_nbr_kernel, rows=seq // GRID_W),
        grid=(batch,),
        in_specs=[_const_spec(bias.shape), spec, spec, spec],
        out_specs=spec,
        out_shape=jax.ShapeDtypeStruct((batch, seq, WIDTH), BF16),
        compiler_params=_params("parallel"),
        name="nbr",
    )(bias, view(q), view(k), view(v))
    return o.reshape(batch * seq, WIDTH)


def _merge_kernel(o1_ref, o2_ref, o3_ref, l1_ref, l2_ref, l3_ref, ob_ref, ga_ref, gb_ref,
                  x_ref, mod_ref, g_ref, wa_ref, wb_ref, wo_ref, out_ref):
    l1, l2, l3 = l1_ref[...], l2_ref[...], l3_ref[...]
    m = jnp.maximum(jnp.maximum(l1, l2), l3)
    e1, e2, e3 = jnp.exp(l1 - m), jnp.exp(l2 - m), jnp.exp(l3 - m)
    mix = (e1 * o1_ref[...].astype(F32) + e2 * o2_ref[...].astype(F32)
           + e3 * o3_ref[...].astype(F32)) / (e1 + e2 + e3)
    ya = jnp.dot(mix.astype(BF16), wa_ref[...], preferred_element_type=F32)
    yb = jnp.dot(ob_ref[...], wb_ref[...], preferred_element_type=F32)
    merged = ga_ref[...].astype(F32) * ya + gb_ref[...].astype(F32) * yb
    z = jnp.dot(merged.astype(BF16), wo_ref[...], preferred_element_type=F32)
    out_ref[...] = x_ref[...] + mod_ref[2:3, :] * _rms(z, g_ref[...])


def _merge(o_pats, l_pats, ob, ga, gb, x2, mod3, g_post, wa, wb, wo, seq, tm):
    t, d = x2.shape
    tps = seq // tm
    row = lambda w: pl.BlockSpec((tm, w), lambda i: (i, 0))
    return pl.pallas_call(
        _merge_kernel,
        grid=(t // tm,),
        in_specs=[row(WIDTH)] * 7 + [row(d)] * 3
                 + [pl.BlockSpec((None, 6, d), lambda i: (i // tps, 0, 0)),
                    _const_spec((1, d)), _const_spec(wa.shape), _const_spec(wb.shape),
                    _const_spec(wo.shape)],
        out_specs=row(d),
        out_shape=jax.ShapeDtypeStruct((t, d), F32),
        compiler_params=_params("parallel"),
        name="merge",
    )(*o_pats, *l_pats, ob, ga, gb, x2, mod3, g_post, wa, wb, wo)


HALO = 16


def _ffn_kernel(x_ref, xp_ref, xn_ref, mod_ref, gpre_ref, gpost_ref, wup_ref, cw_ref, cb_ref,
                wdn_ref, out_ref, uv_ref, ug_ref, *, tps, chunk):
    tm = x_ref.shape[0]
    i = pl.program_id(0)
    has_prev = (i % tps != 0).astype(F32)
    has_next = (i % tps != tps - 1).astype(F32)
    shift, gain, gpre = mod_ref[3:4, :], 1.0 + mod_ref[4:5, :], gpre_ref[...]

    def prenorm(x):
        return _rms(x, gpre) * gain + shift

    x = x_ref[...]
    hext = jnp.concatenate([(prenorm(xp_ref[...]) * has_prev).astype(BF16),
                            prenorm(x).astype(BF16),
                            (prenorm(xn_ref[...]) * has_next).astype(BF16)], axis=0)
    acc = jnp.zeros((tm, D_MODEL), F32)
    for j in range(D_FF // chunk):
        cv = slice(j * chunk, (j + 1) * chunk)
        cg = slice(D_FF + j * chunk, D_FF + (j + 1) * chunk)
        uv_ref[...] = jnp.dot(hext, wup_ref[:, cv], preferred_element_type=F32)
        ug_ref[...] = jnp.dot(hext, wup_ref[:, cg], preferred_element_type=F32)

        def conv(u_ref, cols):
            return (cw_ref[0:1, cols] * u_ref[pl.ds(HALO - 1, tm), :]
                    + cw_ref[1:2, cols] * u_ref[pl.ds(HALO, tm), :]
                    + cw_ref[2:3, cols] * u_ref[pl.ds(HALO + 1, tm), :] + cb_ref[:, cols])

        act = jax.nn.gelu(conv(ug_ref, cg), approximate=True) * conv(uv_ref, cv)
        acc += jnp.dot(act.astype(BF16), wdn_ref[cv, :], preferred_element_type=F32)
    out_ref[...] = x + mod_ref[5:6, :] * _rms(acc, gpost_ref[...])


def _ffn(x1, mod3, g_pre, g_post, w_up, conv_w, conv_b, w_down, seq, tm, chunk):
    t, d = x1.shape
    tps = seq // tm
    hpt = tm // HALO
    last = t // HALO - 1
    return pl.pallas_call(
        functools.partial(_ffn_kernel, tps=tps, chunk=chunk),
        grid=(t // tm,),
        in_specs=[pl.BlockSpec((tm, d), lambda i: (i, 0)),
                  pl.BlockSpec((HALO, d), lambda i: (jnp.maximum(i * hpt - 1, 0), 0)),
                  pl.BlockSpec((HALO, d), lambda i: (jnp.minimum((i + 1) * hpt, last), 0)),
                  pl.BlockSpec((None, 6, d), lambda i: (i // tps, 0, 0)),
                  _const_spec((1, d)), _const_spec((1, d)),
                  _const_spec(w_up.shape), _const_spec(conv_w.shape), _const_spec(conv_b.shape),
                  _const_spec(w_down.shape)],
        out_specs=pl.BlockSpec((tm, d), lambda i: (i, 0)),
        out_shape=jax.ShapeDtypeStruct((t, d), F32),
        scratch_shapes=[pltpu.VMEM((tm + 2 * HALO, chunk), F32)] * 2,
        compiler_params=_params("parallel"),
        name="ffn",
    )(x1, x1, x1, mod3, g_pre, g_post, w_up, conv_w, conv_b, w_down)


def _layer(x, c, p, tables, nbr_bias):
    batch, seq, d = x.shape
    x2 = x.reshape(batch * seq, d)
    mod3 = _modulation(c, p["w_ada"], p["b_ada"]).reshape(batch, 6, d)
    qa, ka, va, qb, kb, vb, ga, gb = _inproj(x2, mod3, p["g_mix_pre"], p["w_in"], tables, seq, 512)
    pats = [_band_attention(qa, ka, va, batch, seq, dil) for dil in DILATIONS]
    ob = _nbr_attention(qb, kb, vb, nbr_bias, batch, seq)
    x1 = _merge([o for o, _ in pats], [l for _, l in pats], ob, ga, gb, x2, mod3,
                p["g_mix_post"], p["w_branch_a"], p["w_branch_b"], p["w_out"], seq, 512)
    y = _ffn(x1, mod3, p["g_ffn_pre"], p["g_ffn_post"], p["w_up"], p["conv_w"], p["conv_b"],
             p["w_down"], seq, 256, D_FF // 2)
    return y.reshape(batch, seq, d)


def kernel(x_prompt, x_sample, c_prompt, c_sample, w_ada, b_ada, g_mix_pre, g_mix_post, g_ffn_pre,
           g_ffn_post, w_in, rpb, w_branch_a, w_branch_b, w_out, w_up, conv_w, conv_b, w_down):
    assert w_ada.shape[0] == 1, "single-layer trunk"
    row = lambda a: a[0].reshape(1, -1)
    p = {
        "w_ada": w_ada[0], "b_ada": b_ada[0],
        "g_mix_pre": row(g_mix_pre), "g_mix_post": row(g_mix_post),
        "g_ffn_pre": row(g_ffn_pre), "g_ffn_post": row(g_ffn_post),
        "w_in": w_in[0].astype(BF16),
        "w_branch_a": w_branch_a[0].astype(BF16), "w_branch_b": w_branch_b[0].astype(BF16),
        "w_out": w_out[0].astype(BF16), "w_up": w_up[0].astype(BF16),
        "conv_w": conv_w[0], "conv_b": row(conv_b), "w_down": w_down[0].astype(BF16),
    }
    nbr_bias = _nbr_bias(rpb[0])
    tables = _rope_tables(x_prompt.shape[1])
    y_prompt = _layer(x_prompt, c_prompt, p, tables, nbr_bias)
    y_sample = _layer(x_sample, c_sample, p, tables, nbr_bias)
    return (y_prompt, y_sample)
```

```python
import functools

import numpy as np
import jax
import jax.numpy as jnp
from jax import lax
from jax.experimental import pallas as pl
from jax.experimental.pallas import tpu as pltpu

D_MODEL = 1024
HEAD_DIM = 64
N_HEADS = 8
WIDTH = N_HEADS * HEAD_DIM
DILATIONS = (1, 4, 16)
BAND_HALF = 64
ROPE_THETA = 500000.0
ROPE_DIM = HEAD_DIM // 4
GRID_W = 64
NA_ROWS = 8
NA_COLS = 16
D_FF = 2816
EPS = 1e-6
NEG_INF = -1e30

LANES = 128
Q_BLOCK = 128
VMEM_LIMIT = 56 * 1024 * 1024

F32 = jnp.float32
BF16 = jnp.bfloat16


def _params(*sem):
    return pltpu.CompilerParams(dimension_semantics=sem, vmem_limit_bytes=VMEM_LIMIT)


def _const_spec(shape):
    nd = len(shape)
    return pl.BlockSpec(shape, lambda *_: (0,) * nd)


def _rms(x, g):
    return x * lax.rsqrt(jnp.mean(x * x, axis=-1, keepdims=True) + EPS) * g


def _mod_kernel(c_ref, w_ref, b_ref, o_ref):
    c = c_ref[...]
    s = c * jax.nn.sigmoid(c)
    w = w_ref[...]
    s_hi = s.astype(BF16)
    s_lo = (s - s_hi.astype(F32)).astype(BF16)
    w_hi = w.astype(BF16)
    w_lo = (w - w_hi.astype(F32)).astype(BF16)
    acc = jnp.dot(s_hi, w_hi, preferred_element_type=F32)
    acc += jnp.dot(s_hi, w_lo, preferred_element_type=F32)
    acc += jnp.dot(s_lo, w_hi, preferred_element_type=F32)
    o_ref[...] = acc + b_ref[...]


def _modulation(c, w_ada, b_ada):
    b, d = c.shape
    n = w_ada.shape[1]
    return pl.pallas_call(
        _mod_kernel,
        grid=(n // d,),
        in_specs=[_const_spec((b, d)),
                  pl.BlockSpec((d, d), lambda j: (0, j)),
                  pl.BlockSpec((1, d), lambda j: (0, j))],
        out_specs=pl.BlockSpec((b, d), lambda j: (0, j)),
        out_shape=jax.ShapeDtypeStruct((b, n), F32),
        compiler_params=_params("arbitrary"),
        name="mod",
    )(c, w_ada, b_ada.reshape(1, n))


def _inproj_kernel(x_ref, mod_ref, g_ref, w_ref, cos_ref, sa_ref, sb_ref,
                   q1_ref, k1_ref, v1_ref, q4_ref, k4_ref, v4_ref, q16_ref, k16_ref, v16_ref,
                   qb_ref, kb_ref, vb_ref, ga_ref, gb_ref, stage_ref):
    tm = x_ref.shape[0]
    h = _rms(x_ref[...], g_ref[...]) * (1.0 + mod_ref[1:2, :]) + mod_ref[0:1, :]
    hb = h.astype(BF16)

    def proj(c0, width):
        return jnp.dot(hb, w_ref[:, c0:c0 + width], preferred_element_type=F32)

    cos, sa, sb = cos_ref[...], sa_ref[...], sb_ref[...]

    def rope(t):
        outs = []
        for g in range(t.shape[1] // LANES):
            tg = t[:, g * LANES:(g + 1) * LANES]
            outs.append(tg * cos + pltpu.roll(tg, ROPE_DIM // 2, 1) * sa
                        + pltpu.roll(tg, LANES - ROPE_DIM // 2, 1) * sb)
        return jnp.concatenate(outs, axis=1)

    def emit(t, refs):
        refs[0][...] = t.astype(BF16)
        for g in range(WIDTH // LANES):
            stage_ref[g] = t[:, g * LANES:(g + 1) * LANES]
        for dil, ref in zip(DILATIONS[1:], refs[1:]):
            for r in range(dil):
                for g in range(WIDTH // LANES):
                    c0 = r * WIDTH + g * LANES
                    ref[:, c0:c0 + LANES] = (
                        stage_ref[g, pl.ds(r, tm // dil, stride=dil), :].astype(BF16))

    scale = HEAD_DIM ** -0.5
    emit(rope(proj(0, WIDTH)) * scale, (q1_ref, q4_ref, q16_ref))
    emit(rope(proj(WIDTH, WIDTH)), (k1_ref, k4_ref, k16_ref))
    emit(proj(2 * WIDTH, WIDTH), (v1_ref, v4_ref, v16_ref))
    qb_ref[...] = (proj(3 * WIDTH, WIDTH) * scale).astype(BF16)
    kb_ref[...] = proj(4 * WIDTH, WIDTH).astype(BF16)
    vb_ref[...] = proj(5 * WIDTH, WIDTH).astype(BF16)
    ga_ref[...] = jax.nn.sigmoid(proj(6 * WIDTH, D_MODEL)).astype(BF16)
    gb_ref[...] = jax.nn.sigmoid(proj(6 * WIDTH + D_MODEL, D_MODEL)).astype(BF16)


def _rope_tables(seq):
    half = ROPE_DIM // 2
    inv = jnp.power(jnp.float32(ROPE_THETA), -jnp.arange(half, dtype=F32) / half)
    ang = jnp.arange(seq).astype(F32)[:, None] * inv[None, :]
    cos, sin = jnp.cos(ang), jnp.sin(ang)
    ones = jnp.ones((seq, HEAD_DIM - ROPE_DIM), F32)
    zeros = jnp.zeros((seq, HEAD_DIM - ROPE_DIM), F32)
    zh = jnp.zeros((seq, half), F32)
    per_head = lambda *parts: jnp.tile(jnp.concatenate(parts, axis=1), (1, LANES // HEAD_DIM))
    return (per_head(cos, cos, ones),
            per_head(zh, sin, zeros),
            per_head(-sin, zh, zeros))


def _inproj(x2, mod3, g_pre, w_in, tables, seq, tm):
    t, d = x2.shape
    tps = seq // tm
    row = lambda w: pl.BlockSpec((tm, w), lambda i: (i, 0))
    tab = pl.BlockSpec((tm, LANES), lambda i: (i % tps, 0))
    dil_shapes = [jax.ShapeDtypeStruct((t // dil, dil * WIDTH), BF16) for dil in DILATIONS]
    dil_specs = [pl.BlockSpec((tm // dil, dil * WIDTH), lambda i: (i, 0)) for dil in DILATIONS]
    outs = pl.pallas_call(
        _inproj_kernel,
        grid=(t // tm,),
        in_specs=[row(d),
                  pl.BlockSpec((None, 6, d), lambda i: (i // tps, 0, 0)),
                  _const_spec((1, d)),
                  _const_spec(w_in.shape),
                  tab, tab, tab],
        out_specs=[s for s in dil_specs for _ in range(3)] + [row(WIDTH)] * 3 + [row(d)] * 2,
        out_shape=[s for s in dil_shapes for _ in range(3)]
                  + [jax.ShapeDtypeStruct((t, WIDTH), BF16)] * 3
                  + [jax.ShapeDtypeStruct((t, d), BF16)] * 2,
        scratch_shapes=[pltpu.VMEM((WIDTH // LANES, tm, LANES), F32)],
        compiler_params=_params("parallel"),
        name="inproj",
    )(x2, mod3, g_pre, w_in, *tables)
    qkv_a = [outs[3 * n:3 * n + 3] for n in range(len(DILATIONS))]
    return qkv_a, outs[9:12], outs[12], outs[13]


def _pair_attention(q2, kwin, vwin, bias_of_head, lo_mask):
    n = kwin.shape[0]
    vaug = jnp.concatenate([vwin, jnp.ones((n, LANES), BF16)], axis=1)
    halves = []
    for hh in range(2):
        keep = lo_mask if hh == 0 else jnp.logical_not(lo_mask)
        qh = jnp.where(keep, q2, jnp.zeros_like(q2))
        s = lax.dot_general(qh, kwin, (((1,), (1,)), ((), ())), preferred_element_type=F32)
        s = s + bias_of_head(hh)
        m = jnp.max(s, axis=1, keepdims=True)
        p = jnp.exp(s - m).astype(BF16)
        r = jnp.dot(p, vaug, preferred_element_type=F32)
        den = r[:, LANES:]
        halves.append((r[:, :LANES] / den, m + jnp.log(den)))
    o = jnp.where(lo_mask, halves[0][0], halves[1][0])
    lse = jnp.where(lo_mask, halves[0][1], halves[1][1])
    return o, lse


def _lo_mask(rows):
    return lax.broadcasted_iota(jnp.int32, (rows, LANES), 1) < HEAD_DIM


def _band_kernel(bias_ref, q_ref, k_ref, v_ref, o_ref, lse_ref, *, length, groups):
    nblk = length // Q_BLOCK
    kw = bias_ref.shape[2]
    lo = _lo_mask(Q_BLOCK)

    def block(blk, g):
        cols = slice(g * LANES, (g + 1) * LANES)
        if nblk == 1:
            i0, ks, kind = 0, 0, 0
        else:
            i0 = pl.multiple_of(blk * Q_BLOCK, Q_BLOCK)
            ks = pl.multiple_of(jnp.clip(i0 - BAND_HALF, 0, length - kw), BAND_HALF)
            kind = jnp.where(blk == 0, 0, jnp.where(blk == nblk - 1, 2, 1))
        bias = bias_ref[kind]
        o, lse = _pair_attention(q_ref[pl.ds(i0, Q_BLOCK), cols],
                                 k_ref[pl.ds(ks, kw), cols],
                                 v_ref[pl.ds(ks, kw), cols],
                                 lambda hh: bias, lo)
        o_ref[pl.ds(i0, Q_BLOCK), cols] = o.astype(BF16)
        lse_ref[pl.ds(i0, Q_BLOCK), cols] = lse

    for g in range(groups):
        if nblk == 1:
            block(0, g)
        else:
            def body(blk, carry, g=g):
                block(blk, g)
                return carry
            lax.fori_loop(0, nblk, body, 0)


def _band_bias(length):
    kw = min(2 * Q_BLOCK, length)
    qi = np.arange(Q_BLOCK)[:, None]
    kj = np.arange(kw)[None, :]
    offsets = (0,) if length == Q_BLOCK else (0, -BAND_HALF, Q_BLOCK - kw)
    tiles = [np.where(np.abs(kj + off - qi) <= BAND_HALF, 0.0, NEG_INF) for off in offsets]
    return jnp.asarray(np.stack(tiles), F32)


def _band_attention(q, k, v, batch, seq, dil):
    length = seq // dil
    width = dil * WIDTH
    bw = min(width, 16 * LANES)
    view = lambda a: a.reshape(batch, length, width)
    spec = pl.BlockSpec((None, length, bw), lambda b, j: (b, 0, j))
    bias = _band_bias(length)
    o, lse = pl.pallas_call(
        functools.partial(_band_kernel, length=length, groups=bw // LANES),
        grid=(batch, width // bw),
        in_specs=[_const_spec(bias.shape), spec, spec, spec],
        out_specs=[spec, spec],
        out_shape=[jax.ShapeDtypeStruct((batch, length, width), BF16),
                   jax.ShapeDtypeStruct((batch, length, width), F32)],
        compiler_params=_params("parallel", "parallel"),
        name=f"band{dil}",
    )(bias, view(q), view(k), view(v))
    return o.reshape(batch * length, width), lse.reshape(batch * length, width)


def _nbr_kernel(bias_ref, q_ref, k_ref, v_ref, o_ref, *, rows):
    lo = _lo_mask(GRID_W)
    nkeys = NA_ROWS * GRID_W

    def row(r, carry):
        rs = jnp.clip(r - NA_ROWS // 2, 0, rows - NA_ROWS)
        var = r - rs
        q0 = pl.multiple_of(r * GRID_W, GRID_W)
        k0 = pl.multiple_of(rs * GRID_W, GRID_W)
        for p in range(WIDTH // LANES):
            cols = slice(p * LANES, (p + 1) * LANES)
            o, _ = _pair_attention(q_ref[pl.ds(q0, GRID_W), cols],
                                   k_ref[pl.ds(k0, nkeys), cols],
                                   v_ref[pl.ds(k0, nkeys), cols],
                                   lambda hh, p=p: bias_ref[(2 * p + hh) * NA_ROWS + var], lo)
            o_ref[pl.ds(q0, GRID_W), cols] = o.astype(BF16)
        return carry

    lax.fori_loop(0, rows, row, 0)


def _nbr_bias(rpb):
    c = np.arange(GRID_W)[:, None]
    ck = np.arange(GRID_W)[None, :]
    cs = np.clip(c - NA_COLS // 2, 0, GRID_W - NA_COLS)
    ok = (ck >= cs) & (ck < cs + NA_COLS)
    pad = GRID_W - NA_COLS
    rp = jnp.pad(rpb.astype(F32), ((0, 0), (0, 0), (pad, pad)))
    t = jnp.stack([rp[:, :, GRID_W - 1 - q:2 * GRID_W - 1 - q] for q in range(GRID_W)], axis=2)
    t = jnp.where(ok[None, None], t, NEG_INF)
    b = jnp.stack([t[:, NA_ROWS - 1 - v:2 * NA_ROWS - 1 - v] for v in range(NA_ROWS)], axis=1)
    b = b.transpose(0, 1, 3, 2, 4)
    return b.reshape(N_HEADS * NA_ROWS, GRID_W, NA_ROWS * GRID_W)


def _nbr_attention(q, k, v, bias, batch, seq):
    view = lambda a: a.reshape(batch, seq, WIDTH)
    spec = pl.BlockSpec((None, seq, WIDTH), lambda b: (b, 0, 0))
    o = pl.pallas_call(
        functools.partial(_nbr_kernel, rows=seq // GRID_W),
        grid=(batch,),
        in_specs=[_const_spec(bias.shape), spec, spec, spec],
        out_specs=spec,
        out_shape=jax.ShapeDtypeStruct((batch, seq, WIDTH), BF16),
        compiler_params=_params("parallel"),
        name="nbr",
    )(bias, view(q), view(k), view(v))
    return o.reshape(batch * seq, WIDTH)


def _merge_kernel(o1_ref, o2_ref, o3_ref, l1_ref, l2_ref, l3_ref, ob_ref, ga_ref, gb_ref,
                  x_ref, mod_ref, g_ref, wa_ref, wb_ref, wo_ref, out_ref, *stage_refs):
    tm = x_ref.shape[0]

    def token_order(ref, dil, stage_ref):
        groups = range(WIDTH // LANES)
        for r in range(dil):
            for g in groups:
                c0 = r * WIDTH + g * LANES
                stage_ref[g, pl.ds(r, tm // dil, stride=dil), :] = ref[:, c0:c0 + LANES].astype(F32)
        return jnp.concatenate([stage_ref[g] for g in groups], axis=1)

    o1, l1 = o1_ref[...].astype(F32), l1_ref[...]
    o2 = token_order(o2_ref, DILATIONS[1], stage_refs[0])
    l2 = token_order(l2_ref, DILATIONS[1], stage_refs[1])
    o3 = token_order(o3_ref, DILATIONS[2], stage_refs[2])
    l3 = token_order(l3_ref, DILATIONS[2], stage_refs[3])
    m = jnp.maximum(jnp.maximum(l1, l2), l3)
    e1, e2, e3 = jnp.exp(l1 - m), jnp.exp(l2 - m), jnp.exp(l3 - m)
    mix = (e1 * o1 + e2 * o2 + e3 * o3) / (e1 + e2 + e3)
    ya = jnp.dot(mix.astype(BF16), wa_ref[...], preferred_element_type=F32)
    yb = jnp.dot(ob_ref[...], wb_ref[...], preferred_element_type=F32)
    merged = ga_ref[...].astype(F32) * ya + gb_ref[...].astype(F32) * yb
    z = jnp.dot(merged.astype(BF16), wo_ref[...], preferred_element_type=F32)
    out_ref[...] = x_ref[...] + mod_ref[2:3, :] * _rms(z, g_ref[...])


def _merge(o_pats, l_pats, ob, ga, gb, x2, mod3, g_post, wa, wb, wo, seq, tm):
    t, d = x2.shape
    tps = seq // tm
    row = lambda w: pl.BlockSpec((tm, w), lambda i: (i, 0))
    pat = [pl.BlockSpec((tm // dil, dil * WIDTH), lambda i: (i, 0)) for dil in DILATIONS]
    return pl.pallas_call(
        _merge_kernel,
        grid=(t // tm,),
        in_specs=pat + pat + [row(WIDTH)] + [row(d)] * 3
                 + [pl.BlockSpec((None, 6, d), lambda i: (i // tps, 0, 0)),
                    _const_spec((1, d)), _const_spec(wa.shape), _const_spec(wb.shape),
                    _const_spec(wo.shape)],
        out_specs=row(d),
        out_shape=jax.ShapeDtypeStruct((t, d), F32),
        scratch_shapes=[pltpu.VMEM((WIDTH // LANES, tm, LANES), F32)] * 4,
        compiler_params=_params("parallel"),
        name="merge",
    )(*o_pats, *l_pats, ob, ga, gb, x2, mod3, g_post, wa, wb, wo)


HALO = 16


def _ffn_kernel(x_ref, xp_ref, xn_ref, mod_ref, gpre_ref, gpost_ref, wup_ref, cw_ref, cb_ref,
                wdn_ref, out_ref, uv_ref, ug_ref, *, tps, chunk):
    tm = x_ref.shape[0]
    i = pl.program_id(0)
    has_prev = (i % tps != 0).astype(F32)
    has_next = (i % tps != tps - 1).astype(F32)
    shift, gain, gpre = mod_ref[3:4, :], 1.0 + mod_ref[4:5, :], gpre_ref[...]

    def prenorm(x):
        return _rms(x, gpre) * gain + shift

    x = x_ref[...]
    hext = jnp.concatenate([(prenorm(xp_ref[...]) * has_prev).astype(BF16),
                            prenorm(x).astype(BF16),
                            (prenorm(xn_ref[...]) * has_next).astype(BF16)], axis=0)
    acc = jnp.zeros((tm, D_MODEL), F32)
    for j in range(D_FF // chunk):
        cv = slice(j * chunk, (j + 1) * chunk)
        cg = slice(D_FF + j * chunk, D_FF + (j + 1) * chunk)
        uv_ref[...] = jnp.dot(hext, wup_ref[:, cv], preferred_element_type=F32)
        ug_ref[...] = jnp.dot(hext, wup_ref[:, cg], preferred_element_type=F32)

        def conv(u_ref, cols):
            return (cw_ref[0:1, cols] * u_ref[pl.ds(HALO - 1, tm), :]
                    + cw_ref[1:2, cols] * u_ref[pl.ds(HALO, tm), :]
                    + cw_ref[2:3, cols] * u_ref[pl.ds(HALO + 1, tm), :] + cb_ref[:, cols])

        act = jax.nn.gelu(conv(ug_ref, cg), approximate=True) * conv(uv_ref, cv)
        acc += jnp.dot(act.astype(BF16), wdn_ref[cv, :], preferred_element_type=F32)
    out_ref[...] = x + mod_ref[5:6, :] * _rms(acc, gpost_ref[...])


def _ffn(x1, mod3, g_pre, g_post, w_up, conv_w, conv_b, w_down, seq, tm, chunk):
    t, d = x1.shape
    tps = seq // tm
    hpt = tm // HALO
    last = t // HALO - 1
    return pl.pallas_call(
        functools.partial(_ffn_kernel, tps=tps, chunk=chunk),
        grid=(t // tm,),
        in_specs=[pl.BlockSpec((tm, d), lambda i: (i, 0)),
                  pl.BlockSpec((HALO, d), lambda i: (jnp.maximum(i * hpt - 1, 0), 0)),
                  pl.BlockSpec((HALO, d), lambda i: (jnp.minimum((i + 1) * hpt, last), 0)),
                  pl.BlockSpec((None, 6, d), lambda i: (i // tps, 0, 0)),
                  _const_spec((1, d)), _const_spec((1, d)),
                  _const_spec(w_up.shape), _const_spec(conv_w.shape), _const_spec(conv_b.shape),
                  _const_spec(w_down.shape)],
        out_specs=pl.BlockSpec((tm, d), lambda i: (i, 0)),
        out_shape=jax.ShapeDtypeStruct((t, d), F32),
        scratch_shapes=[pltpu.VMEM((tm + 2 * HALO, chunk), F32)] * 2,
        compiler_params=_params("parallel"),
        name="ffn",
    )(x1, x1, x1, mod3, g_pre, g_post, w_up, conv_w, conv_b, w_down)


def _layer(x, c, p, tables, nbr_bias):
    batch, seq, d = x.shape
    x2 = x.reshape(batch * seq, d)
    mod3 = _modulation(c, p["w_ada"], p["b_ada"]).reshape(batch, 6, d)
    qkv_a, qkv_b, ga, gb = _inproj(x2, mod3, p["g_mix_pre"], p["w_in"], tables, seq, 512)
    pats = [_band_attention(*qkv, batch, seq, dil) for qkv, dil in zip(qkv_a, DILATIONS)]
    ob = _nbr_attention(*qkv_b, nbr_bias, batch, seq)
    x1 = _merge([o for o, _ in pats], [l for _, l in pats], ob, ga, gb, x2, mod3,
                p["g_mix_post"], p["w_branch_a"], p["w_branch_b"], p["w_out"], seq, 512)
    y = _ffn(x1, mod3, p["g_ffn_pre"], p["g_ffn_post"], p["w_up"], p["conv_w"], p["conv_b"],
             p["w_down"], seq, 256, D_FF // 2)
    return y.reshape(batch, seq, d)


def kernel(x_prompt, x_sample, c_prompt, c_sample, w_ada, b_ada, g_mix_pre, g_mix_post, g_ffn_pre,
           g_ffn_post, w_in, rpb, w_branch_a, w_branch_b, w_out, w_up, conv_w, conv_b, w_down):
    assert w_ada.shape[0] == 1, "single-layer trunk"
    row = lambda a: a[0].reshape(1, -1)
    p = {
        "w_ada": w_ada[0], "b_ada": b_ada[0],
        "g_mix_pre": row(g_mix_pre), "g_mix_post": row(g_mix_post),
        "g_ffn_pre": row(g_ffn_pre), "g_ffn_post": row(g_ffn_post),
        "w_in": w_in[0].astype(BF16),
        "w_branch_a": w_branch_a[0].astype(BF16), "w_branch_b": w_branch_b[0].astype(BF16),
        "w_out": w_out[0].astype(BF16), "w_up": w_up[0].astype(BF16),
        "conv_w": conv_w[0], "conv_b": row(conv_b), "w_down": w_down[0].astype(BF16),
    }
    nbr_bias = _nbr_bias(rpb[0])
    tables = _rope_tables(x_prompt.shape[1])
    y_prompt = _layer(x_prompt, c_prompt, p, tables, nbr_bias)
    y_sample = _layer(x_sample, c_sample, p, tables, nbr_bias)
    return (y_prompt, y_sample)
```

```python
import functools

import numpy as np
import jax
import jax.numpy as jnp
from jax import lax
from jax.experimental import pallas as pl
from jax.experimental.pallas import tpu as pltpu

D_MODEL = 1024
HEAD_DIM = 64
N_HEADS = 8
WIDTH = N_HEADS * HEAD_DIM
DILATIONS = (1, 4, 16)
BAND_HALF = 64
ROPE_THETA = 500000.0
ROPE_DIM = HEAD_DIM // 4
GRID_W = 64
NA_ROWS = 8
NA_COLS = 16
D_FF = 2816
EPS = 1e-6
NEG_INF = -1e30

LANES = 128
Q_BLOCK = 128
GROUPS_PER_STEP = 4
VMEM_LIMIT = 56 * 1024 * 1024

F32 = jnp.float32
BF16 = jnp.bfloat16


def _params(*sem):
    return pltpu.CompilerParams(dimension_semantics=sem, vmem_limit_bytes=VMEM_LIMIT)


def _const_spec(shape):
    nd = len(shape)
    return pl.BlockSpec(shape, lambda *_: (0,) * nd)


def _rms(x, g):
    return x * lax.rsqrt(jnp.mean(x * x, axis=-1, keepdims=True) + EPS) * g


def _mod_kernel(c_ref, w_ref, b_ref, o_ref):
    c = c_ref[...]
    s = c * jax.nn.sigmoid(c)
    w = w_ref[...]
    s_hi = s.astype(BF16)
    s_lo = (s - s_hi.astype(F32)).astype(BF16)
    w_hi = w.astype(BF16)
    w_lo = (w - w_hi.astype(F32)).astype(BF16)
    acc = jnp.dot(s_hi, w_hi, preferred_element_type=F32)
    acc += jnp.dot(s_hi, w_lo, preferred_element_type=F32)
    acc += jnp.dot(s_lo, w_hi, preferred_element_type=F32)
    o_ref[...] = acc + b_ref[...]


def _modulation(c, w_ada, b_ada):
    b, d = c.shape
    n = w_ada.shape[1]
    return pl.pallas_call(
        _mod_kernel,
        grid=(n // d,),
        in_specs=[_const_spec((b, d)),
                  pl.BlockSpec((d, d), lambda j: (0, j)),
                  pl.BlockSpec((1, d), lambda j: (0, j))],
        out_specs=pl.BlockSpec((b, d), lambda j: (0, j)),
        out_shape=jax.ShapeDtypeStruct((b, n), F32),
        compiler_params=_params("arbitrary"),
        name="mod",
    )(c, w_ada, b_ada.reshape(1, n))


def _inproj_kernel(x_ref, mod_ref, g_ref, w_ref, cos_ref, sa_ref, sb_ref,
                   q1_ref, k1_ref, v1_ref, q4_ref, k4_ref, v4_ref, q16_ref, k16_ref, v16_ref,
                   qb_ref, kb_ref, vb_ref, ga_ref, gb_ref, stage_ref):
    tm = x_ref.shape[0]
    h = _rms(x_ref[...], g_ref[...]) * (1.0 + mod_ref[1:2, :]) + mod_ref[0:1, :]
    hb = h.astype(BF16)

    def proj(c0, width):
        return jnp.dot(hb, w_ref[:, c0:c0 + width], preferred_element_type=F32)

    cos, sa, sb = cos_ref[...], sa_ref[...], sb_ref[...]

    def rope(t):
        outs = []
        for g in range(t.shape[1] // LANES):
            tg = t[:, g * LANES:(g + 1) * LANES]
            outs.append(tg * cos + pltpu.roll(tg, ROPE_DIM // 2, 1) * sa
                        + pltpu.roll(tg, LANES - ROPE_DIM // 2, 1) * sb)
        return jnp.concatenate(outs, axis=1)

    def emit(t, refs):
        refs[0][...] = t.astype(BF16)
        for g in range(WIDTH // LANES):
            stage_ref[g] = t[:, g * LANES:(g + 1) * LANES]
        for dil, ref in zip(DILATIONS[1:], refs[1:]):
            for r in range(dil):
                for g in range(WIDTH // LANES):
                    c0 = r * WIDTH + g * LANES
                    ref[:, c0:c0 + LANES] = (
                        stage_ref[g, pl.ds(r, tm // dil, stride=dil), :].astype(BF16))

    scale = HEAD_DIM ** -0.5
    emit(rope(proj(0, WIDTH)) * scale, (q1_ref, q4_ref, q16_ref))
    emit(rope(proj(WIDTH, WIDTH)), (k1_ref, k4_ref, k16_ref))
    emit(proj(2 * WIDTH, WIDTH), (v1_ref, v4_ref, v16_ref))
    qb_ref[...] = (proj(3 * WIDTH, WIDTH) * scale).astype(BF16)
    kb_ref[...] = proj(4 * WIDTH, WIDTH).astype(BF16)
    vb_ref[...] = proj(5 * WIDTH, WIDTH).astype(BF16)
    ga_ref[...] = jax.nn.sigmoid(proj(6 * WIDTH, D_MODEL)).astype(BF16)
    gb_ref[...] = jax.nn.sigmoid(proj(6 * WIDTH + D_MODEL, D_MODEL)).astype(BF16)


def _rope_tables(seq):
    half = ROPE_DIM // 2
    inv = jnp.power(jnp.float32(ROPE_THETA), -jnp.arange(half, dtype=F32) / half)
    ang = jnp.arange(seq).astype(F32)[:, None] * inv[None, :]
    cos, sin = jnp.cos(ang), jnp.sin(ang)
    ones = jnp.ones((seq, HEAD_DIM - ROPE_DIM), F32)
    zeros = jnp.zeros((seq, HEAD_DIM - ROPE_DIM), F32)
    zh = jnp.zeros((seq, half), F32)
    per_head = lambda *parts: jnp.tile(jnp.concatenate(parts, axis=1), (1, LANES // HEAD_DIM))
    return (per_head(cos, cos, ones),
            per_head(zh, sin, zeros),
            per_head(-sin, zh, zeros))


def _inproj(x2, mod3, g_pre, w_in, tables, seq, tm):
    t, d = x2.shape
    tps = seq // tm
    row = lambda w: pl.BlockSpec((tm, w), lambda i: (i, 0))
    tab = pl.BlockSpec((tm, LANES), lambda i: (i % tps, 0))
    dil_shapes = [jax.ShapeDtypeStruct((t // dil, dil * WIDTH), BF16) for dil in DILATIONS]
    dil_specs = [pl.BlockSpec((tm // dil, dil * WIDTH), lambda i: (i, 0)) for dil in DILATIONS]
    outs = pl.pallas_call(
        _inproj_kernel,
        grid=(t // tm,),
        in_specs=[row(d),
                  pl.BlockSpec((None, 6, d), lambda i: (i // tps, 0, 0)),
                  _const_spec((1, d)),
                  _const_spec(w_in.shape),
                  tab, tab, tab],
        out_specs=[s for s in dil_specs for _ in range(3)] + [row(WIDTH)] * 3 + [row(d)] * 2,
        out_shape=[s for s in dil_shapes for _ in range(3)]
                  + [jax.ShapeDtypeStruct((t, WIDTH), BF16)] * 3
                  + [jax.ShapeDtypeStruct((t, d), BF16)] * 2,
        scratch_shapes=[pltpu.VMEM((WIDTH // LANES, tm, LANES), F32)],
        compiler_params=_params("parallel"),
        name="inproj",
    )(x2, mod3, g_pre, w_in, *tables)
    qkv_a = [outs[3 * n:3 * n + 3] for n in range(len(DILATIONS))]
    return qkv_a, outs[9:12], outs[12], outs[13]


def _pair_attention(units, lo_mask, want_lse):
    zero = jnp.zeros_like(units[0][0])
    scores = []
    for q2, kwin, _, bias in units:
        qx = jnp.concatenate([jnp.where(lo_mask, q2, zero), jnp.where(lo_mask, zero, q2)], axis=0)
        s = lax.dot_general(qx, kwin, (((1,), (1,)), ((), ())), preferred_element_type=F32)
        scores.append(s + bias)
    maxes = [jnp.max(s, axis=1, keepdims=True) for s in scores]
    probs = [jnp.exp(s - m).astype(BF16) for s, m in zip(scores, maxes)]
    results = []
    for p, (_, _, vwin, _) in zip(probs, units):
        vaug = jnp.concatenate([vwin, jnp.ones((vwin.shape[0], LANES), BF16)], axis=1)
        results.append(jnp.dot(p, vaug, preferred_element_type=F32))
    outs = []
    for r, m, (q2, _, _, _) in zip(results, maxes, units):
        rows = q2.shape[0]
        den = r[:, LANES:]
        o = r[:, :LANES] / den
        o = jnp.where(lo_mask, o[:rows], o[rows:])
        lse = None
        if want_lse:
            lse = m + jnp.log(den)
            lse = jnp.where(lo_mask, lse[:rows], lse[rows:])
        outs.append((o, lse))
    return outs


def _lo_mask(rows):
    return lax.broadcasted_iota(jnp.int32, (rows, LANES), 1) < HEAD_DIM


def _band_kernel(bias_ref, q_ref, k_ref, v_ref, o_ref, lse_ref, *, length, groups):
    nblk = length // Q_BLOCK
    kw = bias_ref.shape[2]
    lo = _lo_mask(Q_BLOCK)

    def block(blk, g0):
        if nblk == 1:
            i0, ks, kind = 0, 0, 0
        else:
            i0 = pl.multiple_of(blk * Q_BLOCK, Q_BLOCK)
            ks = pl.multiple_of(jnp.clip(i0 - BAND_HALF, 0, length - kw), BAND_HALF)
            kind = jnp.where(blk == 0, 0, jnp.where(blk == nblk - 1, 2, 1))
        bias = bias_ref[kind]
        cols = [slice(g * LANES, (g + 1) * LANES) for g in range(g0, g0 + GROUPS_PER_STEP)]
        units = [(q_ref[pl.ds(i0, Q_BLOCK), c], k_ref[pl.ds(ks, kw), c], v_ref[pl.ds(ks, kw), c], bias)
                 for c in cols]
        for c, (o, lse) in zip(cols, _pair_attention(units, lo, True)):
            o_ref[pl.ds(i0, Q_BLOCK), c] = o.astype(BF16)
            lse_ref[pl.ds(i0, Q_BLOCK), c] = lse

    def body(blk, carry):
        for g0 in range(0, groups, GROUPS_PER_STEP):
            block(blk, g0)
        return carry

    if nblk == 1:
        body(0, 0)
    else:
        lax.fori_loop(0, nblk, body, 0)


def _band_bias(length):
    kw = min(2 * Q_BLOCK, length)
    qi = np.arange(Q_BLOCK)[:, None]
    kj = np.arange(kw)[None, :]
    offsets = (0,) if length == Q_BLOCK else (0, -BAND_HALF, Q_BLOCK - kw)
    tiles = [np.where(np.abs(kj + off - qi) <= BAND_HALF, 0.0, NEG_INF) for off in offsets]
    return jnp.asarray(np.tile(np.stack(tiles), (1, 2, 1)), F32)


def _band_attention(q, k, v, batch, seq, dil):
    length = seq // dil
    width = dil * WIDTH
    bw = min(width, 16 * LANES)
    view = lambda a: a.reshape(batch, length, width)
    spec = pl.BlockSpec((None, length, bw), lambda b, j: (b, 0, j))
    bias = _band_bias(length)
    o, lse = pl.pallas_call(
        functools.partial(_band_kernel, length=length, groups=bw // LANES),
        grid=(batch, width // bw),
        in_specs=[_const_spec(bias.shape), spec, spec, spec],
        out_specs=[spec, spec],
        out_shape=[jax.ShapeDtypeStruct((batch, length, width), BF16),
                   jax.ShapeDtypeStruct((batch, length, width), F32)],
        compiler_params=_params("parallel", "parallel"),
        name=f"band{dil}",
    )(bias, view(q), view(k), view(v))
    return o.reshape(batch * length, width), lse.reshape(batch * length, width)


def _nbr_kernel(bias_ref, q_ref, k_ref, v_ref, o_ref, *, rows):
    lo = _lo_mask(GRID_W)
    nkeys = NA_ROWS * GRID_W

    def row(r, carry):
        rs = jnp.clip(r - NA_ROWS // 2, 0, rows - NA_ROWS)
        var = r - rs
        q0 = pl.multiple_of(r * GRID_W, GRID_W)
        k0 = pl.multiple_of(rs * GRID_W, GRID_W)
        cols = [slice(p * LANES, (p + 1) * LANES) for p in range(WIDTH // LANES)]
        units = [(q_ref[pl.ds(q0, GRID_W), c], k_ref[pl.ds(k0, nkeys), c], v_ref[pl.ds(k0, nkeys), c],
                  bias_ref[p * NA_ROWS + var]) for p, c in enumerate(cols)]
        for c, (o, _) in zip(cols, _pair_attention(units, lo, False)):
            o_ref[pl.ds(q0, GRID_W), c] = o.astype(BF16)
        return carry

    lax.fori_loop(0, rows, row, 0)


def _nbr_bias(rpb):
    c = np.arange(GRID_W)[:, None]
    ck = np.arange(GRID_W)[None, :]
    cs = np.clip(c - NA_COLS // 2, 0, GRID_W - NA_COLS)
    ok = (ck >= cs) & (ck < cs + NA_COLS)
    pad = GRID_W - NA_COLS
    rp = jnp.pad(rpb.astype(F32), ((0, 0), (0, 0), (pad, pad)))
    t = jnp.stack([rp[:, :, GRID_W - 1 - q:2 * GRID_W - 1 - q] for q in range(GRID_W)], axis=2)
    t = jnp.where(ok[None, None], t, NEG_INF)
    b = jnp.stack([t[:, NA_ROWS - 1 - v:2 * NA_ROWS - 1 - v] for v in range(NA_ROWS)], axis=1)
    b = b.transpose(0, 1, 3, 2, 4)
    b = b.reshape(N_HEADS // 2, 2, NA_ROWS, GRID_W, NA_ROWS * GRID_W).transpose(0, 2, 1, 3, 4)
    return b.reshape(N_HEADS // 2 * NA_ROWS, 2 * GRID_W, NA_ROWS * GRID_W)


def _nbr_attention(q, k, v, bias, batch, seq):
    view = lambda a: a.reshape(batch, seq, WIDTH)
    spec = pl.BlockSpec((None, seq, WIDTH), lambda b: (b, 0, 0))
    o = pl.pallas_call(
        functools.partial(_nbr_kernel, rows=seq // GRID_W),
        grid=(batch,),
        in_specs=[_const_spec(bias.shape), spec, spec, spec],
        out_specs=spec,
        out_shape=jax.ShapeDtypeStruct((batch, seq, WIDTH), BF16),
        compiler_params=_params("parallel"),
        name="nbr",
    )(bias, view(q), view(k), view(v))
    return o.reshape(batch * seq, WIDTH)


def _merge_kernel(o1_ref, o2_ref, o3_ref, l1_ref, l2_ref, l3_ref, ob_ref, ga_ref, gb_ref,
                  x_ref, mod_ref, g_ref, wa_ref, wb_ref, wo_ref, out_ref, *stage_refs):
    tm = x_ref.shape[0]

    def token_order(ref, dil, stage_ref):
        groups = range(WIDTH // LANES)
        for r in range(dil):
            for g in groups:
                c0 = r * WIDTH + g * LANES
                stage_ref[g, pl.ds(r, tm // dil, stride=dil), :] = ref[:, c0:c0 + LANES].astype(F32)
        return jnp.concatenate([stage_ref[g] for g in groups], axis=1)

    o1, l1 = o1_ref[...].astype(F32), l1_ref[...]
    o2 = token_order(o2_ref, DILATIONS[1], stage_refs[0])
    l2 = token_order(l2_ref, DILATIONS[1], stage_refs[1])
    o3 = token_order(o3_ref, DILATIONS[2], stage_refs[2])
    l3 = token_order(l3_ref, DILATIONS[2], stage_refs[3])
    m = jnp.maximum(jnp.maximum(l1, l2), l3)
    e1, e2, e3 = jnp.exp(l1 - m), jnp.exp(l2 - m), jnp.exp(l3 - m)
    mix = (e1 * o1 + e2 * o2 + e3 * o3) / (e1 + e2 + e3)
    ya = jnp.dot(mix.astype(BF16), wa_ref[...], preferred_element_type=F32)
    yb = jnp.dot(ob_ref[...], wb_ref[...], preferred_element_type=F32)
    merged = ga_ref[...].astype(F32) * ya + gb_ref[...].astype(F32) * yb
    z = jnp.dot(merged.astype(BF16), wo_ref[...], preferred_element_type=F32)
    out_ref[...] = x_ref[...] + mod_ref[2:3, :] * _rms(z, g_ref[...])


def _merge(o_pats, l_pats, ob, ga, gb, x2, mod3, g_post, wa, wb, wo, seq, tm):
    t, d = x2.shape
    tps = seq // tm
    row = lambda w: pl.BlockSpec((tm, w), lambda i: (i, 0))
    pat = [pl.BlockSpec((tm // dil, dil * WIDTH), lambda i: (i, 0)) for dil in DILATIONS]
    return pl.pallas_call(
        _merge_kernel,
        grid=(t // tm,),
        in_specs=pat + pat + [row(WIDTH)] + [row(d)] * 3
                 + [pl.BlockSpec((None, 6, d), lambda i: (i // tps, 0, 0)),
                    _const_spec((1, d)), _const_spec(wa.shape), _const_spec(wb.shape),
                    _const_spec(wo.shape)],
        out_specs=row(d),
        out_shape=jax.ShapeDtypeStruct((t, d), F32),
        scratch_shapes=[pltpu.VMEM((WIDTH // LANES, tm, LANES), F32)] * 4,
        compiler_params=_params("parallel"),
        name="merge",
    )(*o_pats, *l_pats, ob, ga, gb, x2, mod3, g_post, wa, wb, wo)


HALO = 16


def _ffn_kernel(x_ref, xp_ref, xn_ref, mod_ref, gpre_ref, gpost_ref, wup_ref, cw_ref, cb_ref,
                wdn_ref, out_ref, uv_ref, ug_ref, *, tps, chunk):
    tm = x_ref.shape[0]
    i = pl.program_id(0)
    has_prev = (i % tps != 0).astype(F32)
    has_next = (i % tps != tps - 1).astype(F32)
    shift, gain, gpre = mod_ref[3:4, :], 1.0 + mod_ref[4:5, :], gpre_ref[...]

    def prenorm(x):
        return _rms(x, gpre) * gain + shift

    x = x_ref[...]
    hext = jnp.concatenate([(prenorm(xp_ref[...]) * has_prev).astype(BF16),
                            prenorm(x).astype(BF16),
                            (prenorm(xn_ref[...]) * has_next).astype(BF16)], axis=0)
    acc = jnp.zeros((tm, D_MODEL), F32)
    for j in range(D_FF // chunk):
        cv = slice(j * chunk, (j + 1) * chunk)
        cg = slice(D_FF + j * chunk, D_FF + (j + 1) * chunk)
        uv_ref[...] = jnp.dot(hext, wup_ref[:, cv], preferred_element_type=F32)
        ug_ref[...] = jnp.dot(hext, wup_ref[:, cg], preferred_element_type=F32)

        def conv(u_ref, cols):
            return (cw_ref[0:1, cols] * u_ref[pl.ds(HALO - 1, tm), :]
                    + cw_ref[1:2, cols] * u_ref[pl.ds(HALO, tm), :]
                    + cw_ref[2:3, cols] * u_ref[pl.ds(HALO + 1, tm), :] + cb_ref[:, cols])

        act = jax.nn.gelu(conv(ug_ref, cg), approximate=True) * conv(uv_ref, cv)
        acc += jnp.dot(act.astype(BF16), wdn_ref[cv, :], preferred_element_type=F32)
    out_ref[...] = x + mod_ref[5:6, :] * _rms(acc, gpost_ref[...])


def _ffn(x1, mod3, g_pre, g_post, w_up, conv_w, conv_b, w_down, seq, tm, chunk):
    t, d = x1.shape
    tps = seq // tm
    hpt = tm // HALO
    last = t // HALO - 1
    return pl.pallas_call(
        functools.partial(_ffn_kernel, tps=tps, chunk=chunk),
        grid=(t // tm,),
        in_specs=[pl.BlockSpec((tm, d), lambda i: (i, 0)),
                  pl.BlockSpec((HALO, d), lambda i: (jnp.maximum(i * hpt - 1, 0), 0)),
                  pl.BlockSpec((HALO, d), lambda i: (jnp.minimum((i + 1) * hpt, last), 0)),
                  pl.BlockSpec((None, 6, d), lambda i: (i // tps, 0, 0)),
                  _const_spec((1, d)), _const_spec((1, d)),
                  _const_spec(w_up.shape), _const_spec(conv_w.shape), _const_spec(conv_b.shape),
                  _const_spec(w_down.shape)],
        out_specs=pl.BlockSpec((tm, d), lambda i: (i, 0)),
        out_shape=jax.ShapeDtypeStruct((t, d), F32),
        scratch_shapes=[pltpu.VMEM((tm + 2 * HALO, chunk), F32)] * 2,
        compiler_params=_params("parallel"),
        name="ffn",
    )(x1, x1, x1, mod3, g_pre, g_post, w_up, conv_w, conv_b, w_down)


def _layer(x, c, p, tables, nbr_bias):
    batch, seq, d = x.shape
    x2 = x.reshape(batch * seq, d)
    mod3 = _modulation(c, p["w_ada"], p["b_ada"]).reshape(batch, 6, d)
    qkv_a, qkv_b, ga, gb = _inproj(x2, mod3, p["g_mix_pre"], p["w_in"], tables, seq, 512)
    pats = [_band_attention(*qkv, batch, seq, dil) for qkv, dil in zip(qkv_a, DILATIONS)]
    ob = _nbr_attention(*qkv_b, nbr_bias, batch, seq)
    x1 = _merge([o for o, _ in pats], [l for _, l in pats], ob, ga, gb, x2, mod3,
                p["g_mix_post"], p["w_branch_a"], p["w_branch_b"], p["w_out"], seq, 512)
    y = _ffn(x1, mod3, p["g_ffn_pre"], p["g_ffn_post"], p["w_up"], p["conv_w"], p["conv_b"],
             p["w_down"], seq, 256, D_FF // 2)
    return y.reshape(batch, seq, d)


def kernel(x_prompt, x_sample, c_prompt, c_sample, w_ada, b_ada, g_mix_pre, g_mix_post, g_ffn_pre,
           g_ffn_post, w_in, rpb, w_branch_a, w_branch_b, w_out, w_up, conv_w, conv_b, w_down):
    assert w_ada.shape[0] == 1, "single-layer trunk"
    row = lambda a: a[0].reshape(1, -1)
    p = {
        "w_ada": w_ada[0], "b_ada": b_ada[0],
        "g_mix_pre": row(g_mix_pre), "g_mix_post": row(g_mix_post),
        "g_ffn_pre": row(g_ffn_pre), "g_ffn_post": row(g_ffn_post),
        "w_in": w_in[0].astype(BF16),
        "w_branch_a": w_branch_a[0].astype(BF16), "w_branch_b": w_branch_b[0].astype(BF16),
        "w_out": w_out[0].astype(BF16), "w_up": w_up[0].astype(BF16),
        "conv_w": conv_w[0], "conv_b": row(conv_b), "w_down": w_down[0].astype(BF16),
    }
    nbr_bias = _nbr_bias(rpb[0])
    tables = _rope_tables(x_prompt.shape[1])
    y_prompt = _layer(x_prompt, c_prompt, p, tables, nbr_bias)
    y_sample = _layer(x_sample, c_sample, p, tables, nbr_bias)
    return (y_prompt, y_sample)
```

```python
import functools

import numpy as np
import jax
import jax.numpy as jnp
from jax import lax
from jax.experimental import pallas as pl
from jax.experimental.pallas import tpu as pltpu

D_MODEL = 1024
HEAD_DIM = 64
N_HEADS = 8
WIDTH = N_HEADS * HEAD_DIM
DILATIONS = (1, 4, 16)
BAND_HALF = 64
ROPE_THETA = 500000.0
ROPE_DIM = HEAD_DIM // 4
GRID_W = 64
NA_ROWS = 8
NA_COLS = 16
D_FF = 2816
EPS = 1e-6
NEG_INF = -1e30
LOG2E = 1.4426950408889634

LANES = 128
Q_BLOCK = 128
UNITS_PER_STEP = 8
VMEM_LIMIT = 56 * 1024 * 1024

F32 = jnp.float32
BF16 = jnp.bfloat16


def _params(*sem):
    return pltpu.CompilerParams(dimension_semantics=sem, vmem_limit_bytes=VMEM_LIMIT)


def _const_spec(shape):
    nd = len(shape)
    return pl.BlockSpec(shape, lambda *_: (0,) * nd)


def _rms(x, g):
    return x * lax.rsqrt(jnp.mean(x * x, axis=-1, keepdims=True) + EPS) * g


def _mod_kernel(c_ref, w_ref, b_ref, o_ref):
    c = c_ref[...]
    s = c * jax.nn.sigmoid(c)
    w = w_ref[...]
    s_hi = s.astype(BF16)
    s_lo = (s - s_hi.astype(F32)).astype(BF16)
    w_hi = w.astype(BF16)
    w_lo = (w - w_hi.astype(F32)).astype(BF16)
    acc = jnp.dot(s_hi, w_hi, preferred_element_type=F32)
    acc += jnp.dot(s_hi, w_lo, preferred_element_type=F32)
    acc += jnp.dot(s_lo, w_hi, preferred_element_type=F32)
    o_ref[...] = acc + b_ref[...]


def _modulation(c, w_ada, b_ada):
    b, d = c.shape
    n = w_ada.shape[1]
    return pl.pallas_call(
        _mod_kernel,
        grid=(n // d,),
        in_specs=[_const_spec((b, d)),
                  pl.BlockSpec((d, d), lambda j: (0, j)),
                  pl.BlockSpec((1, d), lambda j: (0, j))],
        out_specs=pl.BlockSpec((b, d), lambda j: (0, j)),
        out_shape=jax.ShapeDtypeStruct((b, n), F32),
        compiler_params=_params("arbitrary"),
        name="mod",
    )(c, w_ada, b_ada.reshape(1, n))


def _inproj_kernel(x_ref, mod_ref, g_ref, w_ref, cos_ref, sa_ref, sb_ref,
                   q1_ref, k1_ref, v1_ref, q4_ref, k4_ref, v4_ref, q16_ref, k16_ref, v16_ref,
                   qb_ref, kb_ref, vb_ref, ga_ref, gb_ref, stage_ref):
    tm = x_ref.shape[0]
    h = _rms(x_ref[...], g_ref[...]) * (1.0 + mod_ref[1:2, :]) + mod_ref[0:1, :]
    hb = h.astype(BF16)

    def proj(c0, width):
        return jnp.dot(hb, w_ref[:, c0:c0 + width], preferred_element_type=F32)

    cos, sa, sb = cos_ref[...], sa_ref[...], sb_ref[...]

    def rope(t):
        outs = []
        for g in range(t.shape[1] // LANES):
            tg = t[:, g * LANES:(g + 1) * LANES]
            outs.append(tg * cos + pltpu.roll(tg, ROPE_DIM // 2, 1) * sa
                        + pltpu.roll(tg, LANES - ROPE_DIM // 2, 1) * sb)
        return jnp.concatenate(outs, axis=1)

    def emit(t, refs):
        refs[0][...] = t.astype(BF16)
        for g in range(WIDTH // LANES):
            stage_ref[g] = t[:, g * LANES:(g + 1) * LANES]
        for dil, ref in zip(DILATIONS[1:], refs[1:]):
            for r in range(dil):
                for g in range(WIDTH // LANES):
                    c0 = r * WIDTH + g * LANES
                    ref[:, c0:c0 + LANES] = (
                        stage_ref[g, pl.ds(r, tm // dil, stride=dil), :].astype(BF16))

    scale = HEAD_DIM ** -0.5 * LOG2E
    emit(rope(proj(0, WIDTH)) * scale, (q1_ref, q4_ref, q16_ref))
    emit(rope(proj(WIDTH, WIDTH)), (k1_ref, k4_ref, k16_ref))
    emit(proj(2 * WIDTH, WIDTH), (v1_ref, v4_ref, v16_ref))
    qb_ref[...] = (proj(3 * WIDTH, WIDTH) * scale).astype(BF16)
    kb_ref[...] = proj(4 * WIDTH, WIDTH).astype(BF16)
    vb_ref[...] = proj(5 * WIDTH, WIDTH).astype(BF16)
    ga_ref[...] = jax.nn.sigmoid(proj(6 * WIDTH, D_MODEL)).astype(BF16)
    gb_ref[...] = jax.nn.sigmoid(proj(6 * WIDTH + D_MODEL, D_MODEL)).astype(BF16)


def _rope_tables(seq):
    half = ROPE_DIM // 2
    inv = jnp.power(jnp.float32(ROPE_THETA), -jnp.arange(half, dtype=F32) / half)
    ang = jnp.arange(seq).astype(F32)[:, None] * inv[None, :]
    cos, sin = jnp.cos(ang), jnp.sin(ang)
    ones = jnp.ones((seq, HEAD_DIM - ROPE_DIM), F32)
    zeros = jnp.zeros((seq, HEAD_DIM - ROPE_DIM), F32)
    zh = jnp.zeros((seq, half), F32)
    per_head = lambda *parts: jnp.tile(jnp.concatenate(parts, axis=1), (1, LANES // HEAD_DIM))
    return (per_head(cos, cos, ones),
            per_head(zh, sin, zeros),
            per_head(-sin, zh, zeros))


def _inproj(x2, mod3, g_pre, w_in, tables, seq, tm):
    t, d = x2.shape
    tps = seq // tm
    row = lambda w: pl.BlockSpec((tm, w), lambda i: (i, 0))
    tab = pl.BlockSpec((tm, LANES), lambda i: (i % tps, 0))
    dil_shapes = [jax.ShapeDtypeStruct((t // dil, dil * WIDTH), BF16) for dil in DILATIONS]
    dil_specs = [pl.BlockSpec((tm // dil, dil * WIDTH), lambda i: (i, 0)) for dil in DILATIONS]
    outs = pl.pallas_call(
        _inproj_kernel,
        grid=(t // tm,),
        in_specs=[row(d),
                  pl.BlockSpec((None, 6, d), lambda i: (i // tps, 0, 0)),
                  _const_spec((1, d)),
                  _const_spec(w_in.shape),
                  tab, tab, tab],
        out_specs=[s for s in dil_specs for _ in range(3)] + [row(WIDTH)] * 3 + [row(d)] * 2,
        out_shape=[s for s in dil_shapes for _ in range(3)]
                  + [jax.ShapeDtypeStruct((t, WIDTH), BF16)] * 3
                  + [jax.ShapeDtypeStruct((t, d), BF16)] * 2,
        scratch_shapes=[pltpu.VMEM((WIDTH // LANES, tm, LANES), F32)],
        compiler_params=_params("parallel"),
        name="inproj",
    )(x2, mod3, g_pre, w_in, *tables)
    qkv_a = [outs[3 * n:3 * n + 3] for n in range(len(DILATIONS))]
    return qkv_a, outs[9:12], outs[12], outs[13]


def _pair_attention(units, lo_mask, normalise):
    zero = jnp.zeros_like(units[0][0])
    scores = []
    for q2, kwin, _, bias in units:
        qx = jnp.concatenate([jnp.where(lo_mask, q2, zero), jnp.where(lo_mask, zero, q2)], axis=0)
        s = lax.dot_general(qx, kwin, (((1,), (1,)), ((), ())), preferred_element_type=F32)
        scores.append(s + bias)
    maxes = [jnp.max(s, axis=1, keepdims=True) for s in scores]
    probs = [jnp.exp2(s - m).astype(BF16) for s, m in zip(scores, maxes)]
    results = []
    for p, (_, _, vwin, _) in zip(probs, units):
        vaug = jnp.concatenate([vwin, jnp.ones((vwin.shape[0], LANES), BF16)], axis=1)
        results.append(jnp.dot(p, vaug, preferred_element_type=F32))
    outs = []
    for r, m, (q2, _, _, _) in zip(results, maxes, units):
        rows = q2.shape[0]
        o, den = r[:, :LANES], r[:, LANES:]
        if normalise:
            o = o / den
        pick = lambda t: jnp.where(lo_mask, t[:rows], t[rows:])
        outs.append((pick(o), pick(jnp.broadcast_to(m, den.shape)), pick(den)))
    return outs


def _lo_mask(rows):
    return lax.broadcasted_iota(jnp.int32, (rows, LANES), 1) < HEAD_DIM


def _band_kernel(bias_ref, q_ref, k_ref, v_ref, o_ref, m_ref, den_ref, *, length, groups):
    nblk = length // Q_BLOCK
    kw = bias_ref.shape[2]
    lo = _lo_mask(Q_BLOCK)

    gps = min(groups, UNITS_PER_STEP)
    bps = min(UNITS_PER_STEP // gps, nblk)

    def step(blk0, g0):
        units, dests = [], []
        for j in range(bps):
            if nblk == 1:
                i0, ks, kind = 0, 0, 0
            else:
                blk = blk0 + j
                i0 = pl.multiple_of(blk * Q_BLOCK, Q_BLOCK)
                ks = pl.multiple_of(jnp.clip(i0 - BAND_HALF, 0, length - kw), BAND_HALF)
                kind = jnp.where(blk == 0, 0, jnp.where(blk == nblk - 1, 2, 1))
            bias = bias_ref[kind]
            for g in range(g0, g0 + gps):
                c = slice(g * LANES, (g + 1) * LANES)
                units.append((q_ref[pl.ds(i0, Q_BLOCK), c], k_ref[pl.ds(ks, kw), c],
                              v_ref[pl.ds(ks, kw), c], bias))
                dests.append((i0, c))
        for (i0, c), (o, m, den) in zip(dests, _pair_attention(units, lo, False)):
            o_ref[pl.ds(i0, Q_BLOCK), c] = o.astype(BF16)
            m_ref[pl.ds(i0, Q_BLOCK), c] = m
            den_ref[pl.ds(i0, Q_BLOCK), c] = den

    def body(it, carry):
        for g0 in range(0, groups, gps):
            step(it * bps, g0)
        return carry

    if nblk == bps:
        body(0, 0)
    else:
        lax.fori_loop(0, nblk // bps, body, 0)


def _band_bias(length):
    kw = min(2 * Q_BLOCK, length)
    qi = np.arange(Q_BLOCK)[:, None]
    kj = np.arange(kw)[None, :]
    offsets = (0,) if length == Q_BLOCK else (0, -BAND_HALF, Q_BLOCK - kw)
    tiles = [np.where(np.abs(kj + off - qi) <= BAND_HALF, 0.0, NEG_INF) for off in offsets]
    return jnp.asarray(np.tile(np.stack(tiles), (1, 2, 1)), F32)


def _band_attention(q, k, v, batch, seq, dil):
    length = seq // dil
    width = dil * WIDTH
    bw = min(width, 16 * LANES)
    view = lambda a: a.reshape(batch, length, width)
    spec = pl.BlockSpec((None, length, bw), lambda b, j: (b, 0, j))
    bias = _band_bias(length)
    outs = pl.pallas_call(
        functools.partial(_band_kernel, length=length, groups=bw // LANES),
        grid=(batch, width // bw),
        in_specs=[_const_spec(bias.shape), spec, spec, spec],
        out_specs=[spec, spec, spec],
        out_shape=[jax.ShapeDtypeStruct((batch, length, width), BF16)]
                  + [jax.ShapeDtypeStruct((batch, length, width), F32)] * 2,
        compiler_params=_params("parallel", "parallel"),
        name=f"band{dil}",
    )(bias, view(q), view(k), view(v))
    return [a.reshape(batch * length, width) for a in outs]


def _nbr_kernel(bias_ref, q_ref, k_ref, v_ref, o_ref, *, rows):
    lo = _lo_mask(GRID_W)
    nkeys = NA_ROWS * GRID_W

    pairs = WIDTH // LANES
    rps = UNITS_PER_STEP // pairs

    def step(it, carry):
        units, dests = [], []
        for j in range(rps):
            r = it * rps + j
            rs = jnp.clip(r - NA_ROWS // 2, 0, rows - NA_ROWS)
            var = r - rs
            q0 = pl.multiple_of(r * GRID_W, GRID_W)
            k0 = pl.multiple_of(rs * GRID_W, GRID_W)
            for p in range(pairs):
                c = slice(p * LANES, (p + 1) * LANES)
                units.append((q_ref[pl.ds(q0, GRID_W), c], k_ref[pl.ds(k0, nkeys), c],
                              v_ref[pl.ds(k0, nkeys), c], bias_ref[p * NA_ROWS + var]))
                dests.append((q0, c))
        for (q0, c), (o, _, _) in zip(dests, _pair_attention(units, lo, True)):
            o_ref[pl.ds(q0, GRID_W), c] = o.astype(BF16)
        return carry

    lax.fori_loop(0, rows // rps, step, 0)


def _nbr_bias(rpb):
    c = np.arange(GRID_W)[:, None]
    ck = np.arange(GRID_W)[None, :]
    cs = np.clip(c - NA_COLS // 2, 0, GRID_W - NA_COLS)
    ok = (ck >= cs) & (ck < cs + NA_COLS)
    pad = GRID_W - NA_COLS
    rp = jnp.pad(rpb.astype(F32) * LOG2E, ((0, 0), (0, 0), (pad, pad)))
    t = jnp.stack([rp[:, :, GRID_W - 1 - q:2 * GRID_W - 1 - q] for q in range(GRID_W)], axis=2)
    t = jnp.where(ok[None, None], t, NEG_INF)
    b = jnp.stack([t[:, NA_ROWS - 1 - v:2 * NA_ROWS - 1 - v] for v in range(NA_ROWS)], axis=1)
    b = b.transpose(0, 1, 3, 2, 4)
    b = b.reshape(N_HEADS // 2, 2, NA_ROWS, GRID_W, NA_ROWS * GRID_W).transpose(0, 2, 1, 3, 4)
    return b.reshape(N_HEADS // 2 * NA_ROWS, 2 * GRID_W, NA_ROWS * GRID_W)


def _nbr_attention(q, k, v, bias, batch, seq):
    view = lambda a: a.reshape(batch, seq, WIDTH)
    spec = pl.BlockSpec((None, seq, WIDTH), lambda b: (b, 0, 0))
    o = pl.pallas_call(
        functools.partial(_nbr_kernel, rows=seq // GRID_W),
        grid=(batch,),
        in_specs=[_const_spec(bias.shape), spec, spec, spec],
        out_specs=spec,
        out_shape=jax.ShapeDtypeStruct((batch, seq, WIDTH), BF16),
        compiler_params=_params("parallel"),
        name="nbr",
    )(bias, view(q), view(k), view(v))
    return o.reshape(batch * seq, WIDTH)


def _merge_kernel(o1_ref, o2_ref, o3_ref, m1_ref, m2_ref, m3_ref, d1_ref, d2_ref, d3_ref,
                  ob_ref, ga_ref, gb_ref,
                  x_ref, mod_ref, g_ref, wa_ref, wb_ref, wo_ref, out_ref, *stage_refs):
    tm = x_ref.shape[0]

    def token_order(ref, dil, stage_ref):
        groups = range(WIDTH // LANES)
        for r in range(dil):
            for g in groups:
                c0 = r * WIDTH + g * LANES
                stage_ref[g, pl.ds(r, tm // dil, stride=dil), :] = ref[:, c0:c0 + LANES].astype(F32)
        return jnp.concatenate([stage_ref[g] for g in groups], axis=1)

    o1, m1, d1 = o1_ref[...].astype(F32), m1_ref[...], d1_ref[...]
    o2, m2, d2 = (token_order(r, DILATIONS[1], s) for r, s in zip((o2_ref, m2_ref, d2_ref), stage_refs[0:3]))
    o3, m3, d3 = (token_order(r, DILATIONS[2], s) for r, s in zip((o3_ref, m3_ref, d3_ref), stage_refs[3:6]))
    m = jnp.maximum(jnp.maximum(m1, m2), m3)
    e1, e2, e3 = jnp.exp2(m1 - m), jnp.exp2(m2 - m), jnp.exp2(m3 - m)
    mix = (e1 * o1 + e2 * o2 + e3 * o3) / (e1 * d1 + e2 * d2 + e3 * d3)
    ya = jnp.dot(mix.astype(BF16), wa_ref[...], preferred_element_type=F32)
    yb = jnp.dot(ob_ref[...], wb_ref[...], preferred_element_type=F32)
    merged = ga_ref[...].astype(F32) * ya + gb_ref[...].astype(F32) * yb
    z = jnp.dot(merged.astype(BF16), wo_ref[...], preferred_element_type=F32)
    out_ref[...] = x_ref[...] + mod_ref[2:3, :] * _rms(z, g_ref[...])


def _merge(o_pats, m_pats, d_pats, ob, ga, gb, x2, mod3, g_post, wa, wb, wo, seq, tm):
    t, d = x2.shape
    tps = seq // tm
    row = lambda w: pl.BlockSpec((tm, w), lambda i: (i, 0))
    pat = [pl.BlockSpec((tm // dil, dil * WIDTH), lambda i: (i, 0)) for dil in DILATIONS]
    return pl.pallas_call(
        _merge_kernel,
        grid=(t // tm,),
        in_specs=pat + pat + pat + [row(WIDTH)] + [row(d)] * 3
                 + [pl.BlockSpec((None, 6, d), lambda i: (i // tps, 0, 0)),
                    _const_spec((1, d)), _const_spec(wa.shape), _const_spec(wb.shape),
                    _const_spec(wo.shape)],
        out_specs=row(d),
        out_shape=jax.ShapeDtypeStruct((t, d), F32),
        scratch_shapes=[pltpu.VMEM((WIDTH // LANES, tm, LANES), F32)] * 6,
        compiler_params=_params("parallel"),
        name="merge",
    )(*o_pats, *m_pats, *d_pats, ob, ga, gb, x2, mod3, g_post, wa, wb, wo)


HALO = 16


def _ffn_kernel(x_ref, xp_ref, xn_ref, mod_ref, gpre_ref, gpost_ref, wup_ref, cw_ref, cb_ref,
                wdn_ref, out_ref, uv_ref, ug_ref, *, tps, chunk):
    tm = x_ref.shape[0]
    i = pl.program_id(0)
    has_prev = (i % tps != 0).astype(F32)
    has_next = (i % tps != tps - 1).astype(F32)
    shift, gain, gpre = mod_ref[3:4, :], 1.0 + mod_ref[4:5, :], gpre_ref[...]

    def prenorm(x):
        return _rms(x, gpre) * gain + shift

    x = x_ref[...]
    hext = jnp.concatenate([(prenorm(xp_ref[...]) * has_prev).astype(BF16),
                            prenorm(x).astype(BF16),
                            (prenorm(xn_ref[...]) * has_next).astype(BF16)], axis=0)
    acc = jnp.zeros((tm, D_MODEL), F32)
    for j in range(D_FF // chunk):
        cv = slice(j * chunk, (j + 1) * chunk)
        cg = slice(D_FF + j * chunk, D_FF + (j + 1) * chunk)
        uv_ref[...] = jnp.dot(hext, wup_ref[:, cv], preferred_element_type=F32)
        ug_ref[...] = jnp.dot(hext, wup_ref[:, cg], preferred_element_type=F32)

        def conv(u_ref, cols):
            return (cw_ref[0:1, cols] * u_ref[pl.ds(HALO - 1, tm), :]
                    + cw_ref[1:2, cols] * u_ref[pl.ds(HALO, tm), :]
                    + cw_ref[2:3, cols] * u_ref[pl.ds(HALO + 1, tm), :] + cb_ref[:, cols])

        act = jax.nn.gelu(conv(ug_ref, cg), approximate=True) * conv(uv_ref, cv)
        acc += jnp.dot(act.astype(BF16), wdn_ref[cv, :], preferred_element_type=F32)
    out_ref[...] = x + mod_ref[5:6, :] * _rms(acc, gpost_ref[...])


def _ffn(x1, mod3, g_pre, g_post, w_up, conv_w, conv_b, w_down, seq, tm, chunk):
    t, d = x1.shape
    tps = seq // tm
    hpt = tm // HALO
    last = t // HALO - 1
    return pl.pallas_call(
        functools.partial(_ffn_kernel, tps=tps, chunk=chunk),
        grid=(t // tm,),
        in_specs=[pl.BlockSpec((tm, d), lambda i: (i, 0)),
                  pl.BlockSpec((HALO, d), lambda i: (jnp.maximum(i * hpt - 1, 0), 0)),
                  pl.BlockSpec((HALO, d), lambda i: (jnp.minimum((i + 1) * hpt, last), 0)),
                  pl.BlockSpec((None, 6, d), lambda i: (i // tps, 0, 0)),
                  _const_spec((1, d)), _const_spec((1, d)),
                  _const_spec(w_up.shape), _const_spec(conv_w.shape), _const_spec(conv_b.shape),
                  _const_spec(w_down.shape)],
        out_specs=pl.BlockSpec((tm, d), lambda i: (i, 0)),
        out_shape=jax.ShapeDtypeStruct((t, d), F32),
        scratch_shapes=[pltpu.VMEM((tm + 2 * HALO, chunk), F32)] * 2,
        compiler_params=_params("parallel"),
        name="ffn",
    )(x1, x1, x1, mod3, g_pre, g_post, w_up, conv_w, conv_b, w_down)


def _layer(x, c, p, tables, nbr_bias):
    batch, seq, d = x.shape
    x2 = x.reshape(batch * seq, d)
    mod3 = _modulation(c, p["w_ada"], p["b_ada"]).reshape(batch, 6, d)
    qkv_a, qkv_b, ga, gb = _inproj(x2, mod3, p["g_mix_pre"], p["w_in"], tables, seq, 512)
    pats = [_band_attention(*qkv, batch, seq, dil) for qkv, dil in zip(qkv_a, DILATIONS)]
    ob = _nbr_attention(*qkv_b, nbr_bias, batch, seq)
    x1 = _merge(*zip(*pats), ob, ga, gb, x2, mod3,
                p["g_mix_post"], p["w_branch_a"], p["w_branch_b"], p["w_out"], seq, 512)
    y = _ffn(x1, mod3, p["g_ffn_pre"], p["g_ffn_post"], p["w_up"], p["conv_w"], p["conv_b"],
             p["w_down"], seq, 256, D_FF // 2)
    return y.reshape(batch, seq, d)


def kernel(x_prompt, x_sample, c_prompt, c_sample, w_ada, b_ada, g_mix_pre, g_mix_post, g_ffn_pre,
           g_ffn_post, w_in, rpb, w_branch_a, w_branch_b, w_out, w_up, conv_w, conv_b, w_down):
    assert w_ada.shape[0] == 1, "single-layer trunk"
    row = lambda a: a[0].reshape(1, -1)
    p = {
        "w_ada": w_ada[0], "b_ada": b_ada[0],
        "g_mix_pre": row(g_mix_pre), "g_mix_post": row(g_mix_post),
        "g_ffn_pre": row(g_ffn_pre), "g_ffn_post": row(g_ffn_post),
        "w_in": w_in[0].astype(BF16),
        "w_branch_a": w_branch_a[0].astype(BF16), "w_branch_b": w_branch_b[0].astype(BF16),
        "w_out": w_out[0].astype(BF16), "w_up": w_up[0].astype(BF16),
        "conv_w": conv_w[0], "conv_b": row(conv_b), "w_down": w_down[0].astype(BF16),
    }
    nbr_bias = _nbr_bias(rpb[0])
    tables = _rope_tables(x_prompt.shape[1])
    y_prompt = _layer(x_prompt, c_prompt, p, tables, nbr_bias)
    y_sample = _layer(x_sample, c_sample, p, tables, nbr_bias)
    return (y_prompt, y_sample)
```

```python
import functools

import numpy as np
import jax
import jax.numpy as jnp
from jax import lax
from jax.experimental import pallas as pl
from jax.experimental.pallas import tpu as pltpu

D_MODEL = 1024
HEAD_DIM = 64
N_HEADS = 8
WIDTH = N_HEADS * HEAD_DIM
DILATIONS = (1, 4, 16)
BAND_HALF = 64
ROPE_THETA = 500000.0
ROPE_DIM = HEAD_DIM // 4
GRID_W = 64
NA_ROWS = 8
NA_COLS = 16
D_FF = 2816
EPS = 1e-6
NEG_INF = -1e30
LOG2E = 1.4426950408889634

LANES = 128
Q_BLOCK = 128
UNITS_PER_STEP = 8
VMEM_LIMIT = 56 * 1024 * 1024

F32 = jnp.float32
BF16 = jnp.bfloat16


def _params(*sem):
    return pltpu.CompilerParams(dimension_semantics=sem, vmem_limit_bytes=VMEM_LIMIT)


def _const_spec(shape):
    nd = len(shape)
    return pl.BlockSpec(shape, lambda *_: (0,) * nd, pipeline_mode=pl.Buffered(1))


def _rms(x, g):
    return x * lax.rsqrt(jnp.mean(x * x, axis=-1, keepdims=True) + EPS) * g


def _mod_kernel(c_ref, w_ref, b_ref, o_ref):
    c = c_ref[...]
    s = c * jax.nn.sigmoid(c)
    w = w_ref[...]
    s_hi = s.astype(BF16)
    s_lo = (s - s_hi.astype(F32)).astype(BF16)
    w_hi = w.astype(BF16)
    w_lo = (w - w_hi.astype(F32)).astype(BF16)
    acc = jnp.dot(s_hi, w_hi, preferred_element_type=F32)
    acc += jnp.dot(s_hi, w_lo, preferred_element_type=F32)
    acc += jnp.dot(s_lo, w_hi, preferred_element_type=F32)
    o_ref[...] = acc + b_ref[...]


def _modulation(c, w_ada, b_ada):
    b, d = c.shape
    n = w_ada.shape[1]
    return pl.pallas_call(
        _mod_kernel,
        grid=(n // d,),
        in_specs=[_const_spec((b, d)),
                  pl.BlockSpec((d, d), lambda j: (0, j)),
                  pl.BlockSpec((1, d), lambda j: (0, j))],
        out_specs=pl.BlockSpec((b, d), lambda j: (0, j)),
        out_shape=jax.ShapeDtypeStruct((b, n), F32),
        compiler_params=_params("arbitrary"),
        name="mod",
    )(c, w_ada, b_ada.reshape(1, n))


def _inproj_kernel(x_ref, mod_ref, g_ref, w_ref, cos_ref, sa_ref, sb_ref,
                   q1_ref, k1_ref, v1_ref, q4_ref, k4_ref, v4_ref, q16_ref, k16_ref, v16_ref,
                   qb_ref, kb_ref, vb_ref, ga_ref, gb_ref, stage_ref):
    tm = x_ref.shape[0]
    h = _rms(x_ref[...], g_ref[...]) * (1.0 + mod_ref[1:2, :]) + mod_ref[0:1, :]
    hb = h.astype(BF16)

    def proj(c0, width):
        return jnp.dot(hb, w_ref[:, c0:c0 + width], preferred_element_type=F32)

    cos, sa, sb = cos_ref[...], sa_ref[...], sb_ref[...]

    def rope(t):
        outs = []
        for g in range(t.shape[1] // LANES):
            tg = t[:, g * LANES:(g + 1) * LANES]
            outs.append(tg * cos + pltpu.roll(tg, ROPE_DIM // 2, 1) * sa
                        + pltpu.roll(tg, LANES - ROPE_DIM // 2, 1) * sb)
        return jnp.concatenate(outs, axis=1)

    def emit(t, refs):
        for g in range(WIDTH // LANES):
            tg = t[:, g * LANES:(g + 1) * LANES]
            refs[0][g] = tg.astype(BF16)
            stage_ref[g] = tg
        for dil, ref in zip(DILATIONS[1:], refs[1:]):
            for r in range(dil):
                for g in range(WIDTH // LANES):
                    ref[g, :, r * LANES:(r + 1) * LANES] = (
                        stage_ref[g, pl.ds(r, tm // dil, stride=dil), :].astype(BF16))

    scale = HEAD_DIM ** -0.5 * LOG2E
    emit(rope(proj(0, WIDTH)) * scale, (q1_ref, q4_ref, q16_ref))
    emit(rope(proj(WIDTH, WIDTH)), (k1_ref, k4_ref, k16_ref))
    emit(proj(2 * WIDTH, WIDTH), (v1_ref, v4_ref, v16_ref))
    qb_ref[...] = (proj(3 * WIDTH, WIDTH) * scale).astype(BF16)
    kb_ref[...] = proj(4 * WIDTH, WIDTH).astype(BF16)
    vb_ref[...] = proj(5 * WIDTH, WIDTH).astype(BF16)
    ga_ref[...] = jax.nn.sigmoid(proj(6 * WIDTH, D_MODEL)).astype(BF16)
    gb_ref[...] = jax.nn.sigmoid(proj(6 * WIDTH + D_MODEL, D_MODEL)).astype(BF16)


def _rope_tables(seq):
    half = ROPE_DIM // 2
    inv = jnp.power(jnp.float32(ROPE_THETA), -jnp.arange(half, dtype=F32) / half)
    ang = jnp.arange(seq).astype(F32)[:, None] * inv[None, :]
    cos, sin = jnp.cos(ang), jnp.sin(ang)
    ones = jnp.ones((seq, HEAD_DIM - ROPE_DIM), F32)
    zeros = jnp.zeros((seq, HEAD_DIM - ROPE_DIM), F32)
    zh = jnp.zeros((seq, half), F32)
    per_head = lambda *parts: jnp.tile(jnp.concatenate(parts, axis=1), (1, LANES // HEAD_DIM))
    return (per_head(cos, cos, ones),
            per_head(zh, sin, zeros),
            per_head(-sin, zh, zeros))


def _inproj(x2, mod3, g_pre, w_in, tables, seq, tm):
    t, d = x2.shape
    tps = seq // tm
    row = lambda w: pl.BlockSpec((tm, w), lambda i: (i, 0))
    tab = pl.BlockSpec((tm, LANES), lambda i: (i % tps, 0))
    pairs = WIDTH // LANES
    dil_shapes = [jax.ShapeDtypeStruct((t // seq, pairs, seq // dil, dil * LANES), BF16)
                  for dil in DILATIONS]
    dil_specs = [pl.BlockSpec((None, pairs, tm // dil, dil * LANES),
                              lambda i: (i // tps, 0, i % tps, 0)) for dil in DILATIONS]
    outs = pl.pallas_call(
        _inproj_kernel,
        grid=(t // tm,),
        in_specs=[row(d),
                  pl.BlockSpec((None, 6, d), lambda i: (i // tps, 0, 0)),
                  _const_spec((1, d)),
                  _const_spec(w_in.shape),
                  tab, tab, tab],
        out_specs=[s for s in dil_specs for _ in range(3)] + [row(WIDTH)] * 3 + [row(d)] * 2,
        out_shape=[s for s in dil_shapes for _ in range(3)]
                  + [jax.ShapeDtypeStruct((t, WIDTH), BF16)] * 3
                  + [jax.ShapeDtypeStruct((t, d), BF16)] * 2,
        scratch_shapes=[pltpu.VMEM((WIDTH // LANES, tm, LANES), F32)],
        compiler_params=_params("parallel"),
        name="inproj",
    )(x2, mod3, g_pre, w_in, *tables)
    qkv_a = [outs[3 * n:3 * n + 3] for n in range(len(DILATIONS))]
    return qkv_a, outs[9:12], outs[12], outs[13]


def _pair_attention(units, lo_mask, normalise):
    zero = jnp.zeros_like(units[0][0])
    scores = []
    for q2, kwin, _, bias in units:
        qx = jnp.concatenate([jnp.where(lo_mask, q2, zero), jnp.where(lo_mask, zero, q2)], axis=0)
        s = lax.dot_general(qx, kwin, (((1,), (1,)), ((), ())), preferred_element_type=F32)
        scores.append(s + bias)
    maxes = [jnp.max(s, axis=1, keepdims=True) for s in scores]
    probs = [jnp.exp2(s - m).astype(BF16) for s, m in zip(scores, maxes)]
    results = []
    for p, (_, _, vwin, _) in zip(probs, units):
        vaug = jnp.concatenate([vwin, jnp.ones((vwin.shape[0], LANES), BF16)], axis=1)
        results.append(jnp.dot(p, vaug, preferred_element_type=F32))
    outs = []
    for r, m, (q2, _, _, _) in zip(results, maxes, units):
        rows = q2.shape[0]
        o, den = r[:, :LANES], r[:, LANES:]
        if normalise:
            o = o / den
        pick = lambda t: jnp.where(lo_mask, t[:rows], t[rows:])
        outs.append((pick(o), pick(jnp.broadcast_to(m, den.shape)), pick(den)))
    return outs


def _lo_mask(rows):
    return lax.broadcasted_iota(jnp.int32, (rows, LANES), 1) < HEAD_DIM


def _band_pattern(bias_ref, q_ref, k_ref, v_ref, dil, sink):
    length = q_ref.shape[0]
    nblk = length // Q_BLOCK
    kw = bias_ref.shape[2]
    lo = _lo_mask(Q_BLOCK)
    gps = min(dil, UNITS_PER_STEP)
    bps = min(UNITS_PER_STEP // gps, nblk)

    def step(blk0, g0):
        units, dests = [], []
        for j in range(bps):
            blk = blk0 + j
            if nblk == 1:
                i0, ks, kind = 0, 0, 0
            else:
                i0 = pl.multiple_of(blk * Q_BLOCK, Q_BLOCK)
                ks = pl.multiple_of(jnp.clip(i0 - BAND_HALF, 0, length - kw), BAND_HALF)
                kind = jnp.where(blk == 0, 0, jnp.where(blk == nblk - 1, 2, 1))
            bias = bias_ref[kind]
            for r in range(g0, g0 + gps):
                c = slice(r * LANES, (r + 1) * LANES)
                units.append((q_ref[pl.ds(i0, Q_BLOCK), c], k_ref[pl.ds(ks, kw), c],
                              v_ref[pl.ds(ks, kw), c], bias))
                dests.append((blk, r))
        for (blk, r), res in zip(dests, _pair_attention(units, lo, False)):
            sink(blk, r, *res)

    def body(it, carry):
        for g0 in range(0, dil, gps):
            step(it * bps, g0)
        return carry

    if nblk == bps:
        body(0, 0)
    else:
        lax.fori_loop(0, nblk // bps, body, 0)


def _dilated_kernel(b1_ref, b4_ref, b16_ref, q1_ref, k1_ref, v1_ref, q4_ref, k4_ref, v4_ref,
                    q16_ref, k16_ref, v16_ref, out_ref, o_s, m_s, d_s):
    for slot, (dil, refs) in enumerate(((DILATIONS[1], (b4_ref, q4_ref, k4_ref, v4_ref)),
                                        (DILATIONS[2], (b16_ref, q16_ref, k16_ref, v16_ref)))):
        def scatter(blk, r, o, m, den, slot=slot, dil=dil):
            rows = pl.ds(blk * (Q_BLOCK * dil) + r, Q_BLOCK, stride=dil)
            o_s[slot, rows, :] = o
            m_s[slot, rows, :] = m
            d_s[slot, rows, :] = den
        _band_pattern(*refs, dil, scatter)

    def mix(blk, r, o1, m1, d1):
        rows = pl.ds(pl.multiple_of(blk * Q_BLOCK, Q_BLOCK), Q_BLOCK)
        o2, m2, d2 = o_s[0, rows, :], m_s[0, rows, :], d_s[0, rows, :]
        o3, m3, d3 = o_s[1, rows, :], m_s[1, rows, :], d_s[1, rows, :]
        m = jnp.maximum(jnp.maximum(m1, m2), m3)
        e1, e2, e3 = jnp.exp2(m1 - m), jnp.exp2(m2 - m), jnp.exp2(m3 - m)
        out_ref[rows, :] = ((e1 * o1 + e2 * o2 + e3 * o3)
                            / (e1 * d1 + e2 * d2 + e3 * d3)).astype(BF16)

    _band_pattern(b1_ref, q1_ref, k1_ref, v1_ref, DILATIONS[0], mix)


def _band_bias(length):
    kw = min(2 * Q_BLOCK, length)
    qi = np.arange(Q_BLOCK)[:, None]
    kj = np.arange(kw)[None, :]
    offsets = (0,) if length == Q_BLOCK else (0, -BAND_HALF, Q_BLOCK - kw)
    tiles = [np.where(np.abs(kj + off - qi) <= BAND_HALF, 0.0, NEG_INF) for off in offsets]
    return jnp.asarray(np.tile(np.stack(tiles), (1, 2, 1)), F32)


def _dilated_attention(qkv_a, batch, seq):
    pairs = WIDTH // LANES
    biases = [_band_bias(seq // dil) for dil in DILATIONS]
    spec = lambda dil: pl.BlockSpec((None, None, seq // dil, dil * LANES), lambda b, g: (b, g, 0, 0))
    return pl.pallas_call(
        _dilated_kernel,
        grid=(batch, pairs),
        in_specs=[_const_spec(b.shape) for b in biases]
                 + [spec(dil) for dil in DILATIONS for _ in range(3)],
        out_specs=pl.BlockSpec((None, seq, LANES), lambda b, g: (b, 0, g)),
        out_shape=jax.ShapeDtypeStruct((batch, seq, WIDTH), BF16),
        scratch_shapes=[pltpu.VMEM((len(DILATIONS) - 1, seq, LANES), F32)] * 3,
        compiler_params=_params("parallel", "parallel"),
        name="dilated",
    )(*biases, *[a for qkv in qkv_a for a in qkv])


def _nbr_kernel(bias_ref, q_ref, k_ref, v_ref, o_ref, *, rows):
    lo = _lo_mask(GRID_W)
    nkeys = NA_ROWS * GRID_W

    pairs = WIDTH // LANES
    rps = UNITS_PER_STEP // pairs

    def step(it, carry):
        units, dests = [], []
        for j in range(rps):
            r = it * rps + j
            rs = jnp.clip(r - NA_ROWS // 2, 0, rows - NA_ROWS)
            var = r - rs
            q0 = pl.multiple_of(r * GRID_W, GRID_W)
            k0 = pl.multiple_of(rs * GRID_W, GRID_W)
            for p in range(pairs):
                c = slice(p * LANES, (p + 1) * LANES)
                units.append((q_ref[pl.ds(q0, GRID_W), c], k_ref[pl.ds(k0, nkeys), c],
                              v_ref[pl.ds(k0, nkeys), c], bias_ref[p * NA_ROWS + var]))
                dests.append((q0, c))
        for (q0, c), (o, _, _) in zip(dests, _pair_attention(units, lo, True)):
            o_ref[pl.ds(q0, GRID_W), c] = o.astype(BF16)
        return carry

    lax.fori_loop(0, rows // rps, step, 0)


def _nbr_bias(rpb):
    c = np.arange(GRID_W)[:, None]
    ck = np.arange(GRID_W)[None, :]
    cs = np.clip(c - NA_COLS // 2, 0, GRID_W - NA_COLS)
    ok = (ck >= cs) & (ck < cs + NA_COLS)
    pad = GRID_W - NA_COLS
    rp = jnp.pad(rpb.astype(F32) * LOG2E, ((0, 0), (0, 0), (pad, pad)))
    t = jnp.stack([rp[:, :, GRID_W - 1 - q:2 * GRID_W - 1 - q] for q in range(GRID_W)], axis=2)
    t = jnp.where(ok[None, None], t, NEG_INF)
    b = jnp.stack([t[:, NA_ROWS - 1 - v:2 * NA_ROWS - 1 - v] for v in range(NA_ROWS)], axis=1)
    b = b.transpose(0, 1, 3, 2, 4)
    b = b.reshape(N_HEADS // 2, 2, NA_ROWS, GRID_W, NA_ROWS * GRID_W).transpose(0, 2, 1, 3, 4)
    return b.reshape(N_HEADS // 2 * NA_ROWS, 2 * GRID_W, NA_ROWS * GRID_W)


def _nbr_attention(q, k, v, bias, batch, seq):
    view = lambda a: a.reshape(batch, seq, WIDTH)
    spec = pl.BlockSpec((None, seq, WIDTH), lambda b: (b, 0, 0))
    o = pl.pallas_call(
        functools.partial(_nbr_kernel, rows=seq // GRID_W),
        grid=(batch,),
        in_specs=[_const_spec(bias.shape), spec, spec, spec],
        out_specs=spec,
        out_shape=jax.ShapeDtypeStruct((batch, seq, WIDTH), BF16),
        compiler_params=_params("parallel"),
        name="nbr",
    )(bias, view(q), view(k), view(v))
    return o.reshape(batch * seq, WIDTH)


def _merge_kernel(oa_ref, ob_ref, ga_ref, gb_ref,
                  x_ref, mod_ref, g_ref, wa_ref, wb_ref, wo_ref, out_ref):
    ya = jnp.dot(oa_ref[...], wa_ref[...], preferred_element_type=F32)
    yb = jnp.dot(ob_ref[...], wb_ref[...], preferred_element_type=F32)
    merged = ga_ref[...].astype(F32) * ya + gb_ref[...].astype(F32) * yb
    z = jnp.dot(merged.astype(BF16), wo_ref[...], preferred_element_type=F32)
    out_ref[...] = x_ref[...] + mod_ref[2:3, :] * _rms(z, g_ref[...])


def _merge(oa, ob, ga, gb, x2, mod3, g_post, wa, wb, wo, seq, tm):
    t, d = x2.shape
    tps = seq // tm
    row = lambda w: pl.BlockSpec((tm, w), lambda i: (i, 0))
    return pl.pallas_call(
        _merge_kernel,
        grid=(t // tm,),
        in_specs=[row(WIDTH)] * 2 + [row(d)] * 3
                 + [pl.BlockSpec((None, 6, d), lambda i: (i // tps, 0, 0)),
                    _const_spec((1, d)), _const_spec(wa.shape), _const_spec(wb.shape),
                    _const_spec(wo.shape)],
        out_specs=row(d),
        out_shape=jax.ShapeDtypeStruct((t, d), F32),
        compiler_params=_params("parallel"),
        name="merge",
    )(oa, ob, ga, gb, x2, mod3, g_post, wa, wb, wo)


HALO = 16
FFN_TM = 512
FFN_CHUNKS = (256, 768, 768, 768, 256)


def _ffn_kernel(x_ref, xp_ref, xn_ref, mod_ref, gpre_ref, gpost_ref, wup_ref, cw_ref, cb_ref,
                wdn_ref, out_ref, uv_ref, ug_ref, *, tps, chunks):
    tm = x_ref.shape[0]
    i = pl.program_id(0)
    has_prev = (i % tps != 0).astype(F32)
    has_next = (i % tps != tps - 1).astype(F32)
    shift, gain, gpre = mod_ref[3:4, :], 1.0 + mod_ref[4:5, :], gpre_ref[...]

    def prenorm(x):
        return _rms(x, gpre) * gain + shift

    x = x_ref[...]
    hext = jnp.concatenate([(prenorm(xp_ref[...]) * has_prev).astype(BF16),
                            prenorm(x).astype(BF16),
                            (prenorm(xn_ref[...]) * has_next).astype(BF16)], axis=0)
    starts = [sum(chunks[:j]) for j in range(len(chunks))]

    def up(j):
        c0, w = starts[j], chunks[j]
        uv_ref[j % 2, :, 0:w] = jnp.dot(hext, wup_ref[:, c0:c0 + w], preferred_element_type=F32)
        ug_ref[j % 2, :, 0:w] = jnp.dot(hext, wup_ref[:, D_FF + c0:D_FF + c0 + w],
                                        preferred_element_type=F32)

    def activation(j):
        c0, w = starts[j], chunks[j]

        def conv(u_ref, col0):
            cols = slice(col0, col0 + w)
            return (cw_ref[0:1, cols] * u_ref[j % 2, pl.ds(HALO - 1, tm), 0:w]
                    + cw_ref[1:2, cols] * u_ref[j % 2, pl.ds(HALO, tm), 0:w]
                    + cw_ref[2:3, cols] * u_ref[j % 2, pl.ds(HALO + 1, tm), 0:w] + cb_ref[:, cols])

        return (jax.nn.gelu(conv(ug_ref, D_FF + c0), approximate=True) * conv(uv_ref, c0)).astype(BF16)

    up(0)
    acc = jnp.zeros((tm, D_MODEL), F32)
    for j in range(len(chunks)):
        if j + 1 < len(chunks):
            up(j + 1)
        c0, w = starts[j], chunks[j]
        acc += jnp.dot(activation(j), wdn_ref[c0:c0 + w, :], preferred_element_type=F32)
    out_ref[...] = x + mod_ref[5:6, :] * _rms(acc, gpost_ref[...])


def _ffn(x1, mod3, g_pre, g_post, w_up, conv_w, conv_b, w_down, seq, tm, chunks):
    assert sum(chunks) == D_FF and all(c % LANES == 0 for c in chunks)
    t, d = x1.shape
    tps = seq // tm
    hpt = tm // HALO
    last = t // HALO - 1
    return pl.pallas_call(
        functools.partial(_ffn_kernel, tps=tps, chunks=chunks),
        grid=(t // tm,),
        in_specs=[pl.BlockSpec((tm, d), lambda i: (i, 0)),
                  pl.BlockSpec((HALO, d), lambda i: (jnp.maximum(i * hpt - 1, 0), 0)),
                  pl.BlockSpec((HALO, d), lambda i: (jnp.minimum((i + 1) * hpt, last), 0)),
                  pl.BlockSpec((None, 6, d), lambda i: (i // tps, 0, 0)),
                  _const_spec((1, d)), _const_spec((1, d)),
                  _const_spec(w_up.shape), _const_spec(conv_w.shape), _const_spec(conv_b.shape),
                  _const_spec(w_down.shape)],
        out_specs=pl.BlockSpec((tm, d), lambda i: (i, 0)),
        out_shape=jax.ShapeDtypeStruct((t, d), F32),
        scratch_shapes=[pltpu.VMEM((2, tm + 2 * HALO, max(chunks)), F32)] * 2,
        compiler_params=_params("parallel"),
        name="ffn",
    )(x1, x1, x1, mod3, g_pre, g_post, w_up, conv_w, conv_b, w_down)


def _layer(x, c, p, tables, nbr_bias):
    batch, seq, d = x.shape
    x2 = x.reshape(batch * seq, d)
    mod3 = _modulation(c, p["w_ada"], p["b_ada"]).reshape(batch, 6, d)
    qkv_a, qkv_b, ga, gb = _inproj(x2, mod3, p["g_mix_pre"], p["w_in"], tables, seq, 512)
    oa = _dilated_attention(qkv_a, batch, seq).reshape(batch * seq, WIDTH)
    ob = _nbr_attention(*qkv_b, nbr_bias, batch, seq)
    x1 = _merge(oa, ob, ga, gb, x2, mod3,
                p["g_mix_post"], p["w_branch_a"], p["w_branch_b"], p["w_out"], seq, 512)
    y = _ffn(x1, mod3, p["g_ffn_pre"], p["g_ffn_post"], p["w_up"], p["conv_w"], p["conv_b"],
             p["w_down"], seq, FFN_TM, FFN_CHUNKS)
    return y.reshape(batch, seq, d)


def kernel(x_prompt, x_sample, c_prompt, c_sample, w_ada, b_ada, g_mix_pre, g_mix_post, g_ffn_pre,
           g_ffn_post, w_in, rpb, w_branch_a, w_branch_b, w_out, w_up, conv_w, conv_b, w_down):
    assert w_ada.shape[0] == 1, "single-layer trunk"
    row = lambda a: a[0].reshape(1, -1)
    p = {
        "w_ada": w_ada[0], "b_ada": b_ada[0],
        "g_mix_pre": row(g_mix_pre), "g_mix_post": row(g_mix_post),
        "g_ffn_pre": row(g_ffn_pre), "g_ffn_post": row(g_ffn_post),
        "w_in": w_in[0].astype(BF16),
        "w_branch_a": w_branch_a[0].astype(BF16), "w_branch_b": w_branch_b[0].astype(BF16),
        "w_out": w_out[0].astype(BF16), "w_up": w_up[0].astype(BF16),
        "conv_w": conv_w[0], "conv_b": row(conv_b), "w_down": w_down[0].astype(BF16),
    }
    nbr_bias = _nbr_bias(rpb[0])
    tables = _rope_tables(x_prompt.shape[1])
    y_prompt = _layer(x_prompt, c_prompt, p, tables, nbr_bias)
    y_sample = _layer(x_sample, c_sample, p, tables, nbr_bias)
    return (y_prompt, y_sample)
```

```python
import functools

import numpy as np
import jax
import jax.numpy as jnp
from jax import lax
from jax.experimental import pallas as pl
from jax.experimental.pallas import tpu as pltpu

D_MODEL = 1024
HEAD_DIM = 64
N_HEADS = 8
WIDTH = N_HEADS * HEAD_DIM
DILATIONS = (1, 4, 16)
BAND_HALF = 64
ROPE_THETA = 500000.0
ROPE_DIM = HEAD_DIM // 4
GRID_W = 64
NA_ROWS = 8
NA_COLS = 16
D_FF = 2816
EPS = 1e-6
NEG_INF = -1e30
LOG2E = 1.4426950408889634

LANES = 128
Q_BLOCK = 128
UNITS_PER_STEP = 16
VMEM_LIMIT = 56 * 1024 * 1024

F32 = jnp.float32
BF16 = jnp.bfloat16


def _params(*sem):
    return pltpu.CompilerParams(dimension_semantics=sem, vmem_limit_bytes=VMEM_LIMIT)


def _const_spec(shape):
    nd = len(shape)
    return pl.BlockSpec(shape, lambda *_: (0,) * nd, pipeline_mode=pl.Buffered(1))


def _rms(x, g):
    return x * lax.rsqrt(jnp.mean(x * x, axis=-1, keepdims=True) + EPS) * g


def _mod_kernel(c_ref, w_ref, b_ref, o_ref):
    c = c_ref[...]
    s = c * jax.nn.sigmoid(c)
    w = w_ref[...]
    s_hi = s.astype(BF16)
    s_lo = (s - s_hi.astype(F32)).astype(BF16)
    w_hi = w.astype(BF16)
    w_lo = (w - w_hi.astype(F32)).astype(BF16)
    acc = jnp.dot(s_hi, w_hi, preferred_element_type=F32)
    acc += jnp.dot(s_hi, w_lo, preferred_element_type=F32)
    acc += jnp.dot(s_lo, w_hi, preferred_element_type=F32)
    o_ref[...] = acc + b_ref[...]


def _modulation(c, w_ada, b_ada):
    b, d = c.shape
    n = w_ada.shape[1]
    return pl.pallas_call(
        _mod_kernel,
        grid=(n // d,),
        in_specs=[_const_spec((b, d)),
                  pl.BlockSpec((d, d), lambda j: (0, j)),
                  pl.BlockSpec((1, d), lambda j: (0, j))],
        out_specs=pl.BlockSpec((b, d), lambda j: (0, j)),
        out_shape=jax.ShapeDtypeStruct((b, n), F32),
        compiler_params=_params("arbitrary"),
        name="mod",
    )(c, w_ada, b_ada.reshape(1, n))


def _inproj_kernel(x_ref, mod_ref, g_ref, w_ref, cos_ref, sa_ref, sb_ref,
                   q1_ref, k1_ref, v1_ref, q4_ref, k4_ref, v4_ref, q16_ref, k16_ref, v16_ref,
                   qb_ref, kb_ref, vb_ref, ga_ref, gb_ref, stage_ref):
    tm = x_ref.shape[0]
    h = _rms(x_ref[...], g_ref[...]) * (1.0 + mod_ref[1:2, :]) + mod_ref[0:1, :]
    hb = h.astype(BF16)

    def proj(c0, width):
        return jnp.dot(hb, w_ref[:, c0:c0 + width], preferred_element_type=F32)

    cos, sa, sb = cos_ref[...], sa_ref[...], sb_ref[...]

    def rope(t):
        outs = []
        for g in range(t.shape[1] // LANES):
            tg = t[:, g * LANES:(g + 1) * LANES]
            outs.append(tg * cos + pltpu.roll(tg, ROPE_DIM // 2, 1) * sa
                        + pltpu.roll(tg, LANES - ROPE_DIM // 2, 1) * sb)
        return jnp.concatenate(outs, axis=1)

    def emit(t, refs):
        for g in range(WIDTH // LANES):
            tg = t[:, g * LANES:(g + 1) * LANES]
            refs[0][g] = tg.astype(BF16)
            stage_ref[g] = tg
        for dil, ref in zip(DILATIONS[1:], refs[1:]):
            for r in range(dil):
                for g in range(WIDTH // LANES):
                    ref[g, :, r * LANES:(r + 1) * LANES] = (
                        stage_ref[g, pl.ds(r, tm // dil, stride=dil), :].astype(BF16))

    scale = HEAD_DIM ** -0.5 * LOG2E
    emit(rope(proj(0, WIDTH)) * scale, (q1_ref, q4_ref, q16_ref))
    emit(rope(proj(WIDTH, WIDTH)), (k1_ref, k4_ref, k16_ref))
    emit(proj(2 * WIDTH, WIDTH), (v1_ref, v4_ref, v16_ref))
    qb_ref[...] = (proj(3 * WIDTH, WIDTH) * scale).astype(BF16)
    kb_ref[...] = proj(4 * WIDTH, WIDTH).astype(BF16)
    vb_ref[...] = proj(5 * WIDTH, WIDTH).astype(BF16)
    ga_ref[...] = jax.nn.sigmoid(proj(6 * WIDTH, D_MODEL)).astype(BF16)
    gb_ref[...] = jax.nn.sigmoid(proj(6 * WIDTH + D_MODEL, D_MODEL)).astype(BF16)


def _rope_tables(seq):
    half = ROPE_DIM // 2
    inv = jnp.power(jnp.float32(ROPE_THETA), -jnp.arange(half, dtype=F32) / half)
    ang = jnp.arange(seq).astype(F32)[:, None] * inv[None, :]
    cos, sin = jnp.cos(ang), jnp.sin(ang)
    ones = jnp.ones((seq, HEAD_DIM - ROPE_DIM), F32)
    zeros = jnp.zeros((seq, HEAD_DIM - ROPE_DIM), F32)
    zh = jnp.zeros((seq, half), F32)
    per_head = lambda *parts: jnp.tile(jnp.concatenate(parts, axis=1), (1, LANES // HEAD_DIM))
    return (per_head(cos, cos, ones),
            per_head(zh, sin, zeros),
            per_head(-sin, zh, zeros))


def _inproj(x2, mod3, g_pre, w_in, tables, seq, tm):
    t, d = x2.shape
    tps = seq // tm
    row = lambda w: pl.BlockSpec((tm, w), lambda i: (i, 0))
    tab = pl.BlockSpec((tm, LANES), lambda i: (i % tps, 0))
    pairs = WIDTH // LANES
    dil_shapes = [jax.ShapeDtypeStruct((t // seq, pairs, seq // dil, dil * LANES), BF16)
                  for dil in DILATIONS]
    dil_specs = [pl.BlockSpec((None, pairs, tm // dil, dil * LANES),
                              lambda i: (i // tps, 0, i % tps, 0)) for dil in DILATIONS]
    outs = pl.pallas_call(
        _inproj_kernel,
        grid=(t // tm,),
        in_specs=[row(d),
                  pl.BlockSpec((None, 6, d), lambda i: (i // tps, 0, 0)),
                  _const_spec((1, d)),
                  _const_spec(w_in.shape),
                  tab, tab, tab],
        out_specs=[s for s in dil_specs for _ in range(3)] + [row(WIDTH)] * 3 + [row(d)] * 2,
        out_shape=[s for s in dil_shapes for _ in range(3)]
                  + [jax.ShapeDtypeStruct((t, WIDTH), BF16)] * 3
                  + [jax.ShapeDtypeStruct((t, d), BF16)] * 2,
        scratch_shapes=[pltpu.VMEM((WIDTH // LANES, tm, LANES), F32)],
        compiler_params=_params("parallel"),
        name="inproj",
    )(x2, mod3, g_pre, w_in, *tables)
    qkv_a = [outs[3 * n:3 * n + 3] for n in range(len(DILATIONS))]
    return qkv_a, outs[9:12], outs[12], outs[13]


def _pair_attention(units, lo_mask, normalise):
    zero = jnp.zeros_like(units[0][0])
    scores = []
    for q2, kwin, _, bias in units:
        qx = jnp.concatenate([jnp.where(lo_mask, q2, zero), jnp.where(lo_mask, zero, q2)], axis=0)
        s = lax.dot_general(qx, kwin, (((1,), (1,)), ((), ())), preferred_element_type=F32)
        scores.append(s + bias)
    maxes = [jnp.max(s, axis=1, keepdims=True) for s in scores]
    probs = [jnp.exp2(s - m).astype(BF16) for s, m in zip(scores, maxes)]
    results = []
    for p, (_, _, vwin, _) in zip(probs, units):
        vaug = jnp.concatenate([vwin, jnp.ones((vwin.shape[0], LANES), BF16)], axis=1)
        results.append(jnp.dot(p, vaug, preferred_element_type=F32))
    outs = []
    for r, m, (q2, _, _, _) in zip(results, maxes, units):
        rows = q2.shape[0]
        o, den = r[:, :LANES], r[:, LANES:]
        if normalise:
            o = o / den
        pick = lambda t: jnp.where(lo_mask, t[:rows], t[rows:])
        outs.append((pick(o), pick(jnp.broadcast_to(m, den.shape)), pick(den)))
    return outs


def _lo_mask(rows):
    return lax.broadcasted_iota(jnp.int32, (rows, LANES), 1) < HEAD_DIM


def _band_pattern(bias_ref, q_ref, k_ref, v_ref, dil, sink):
    length = q_ref.shape[0]
    nblk = length // Q_BLOCK
    kw = bias_ref.shape[2]
    lo = _lo_mask(Q_BLOCK)
    gps = min(dil, UNITS_PER_STEP)
    bps = min(UNITS_PER_STEP // gps, nblk)

    def step(blk0, g0):
        units, dests = [], []
        for j in range(bps):
            blk = blk0 + j
            if isinstance(blk, int):
                i0 = blk * Q_BLOCK
                ks = min(max(i0 - BAND_HALF, 0), length - kw)
                kind = 0 if blk == 0 else (2 if blk == nblk - 1 else 1)
                kind = min(kind, bias_ref.shape[0] - 1)
            else:
                i0 = pl.multiple_of(blk * Q_BLOCK, Q_BLOCK)
                ks = pl.multiple_of(jnp.clip(i0 - BAND_HALF, 0, length - kw), BAND_HALF)
                kind = jnp.where(blk == 0, 0, jnp.where(blk == nblk - 1, 2, 1))
            bias = bias_ref[kind]
            for r in range(g0, g0 + gps):
                c = slice(r * LANES, (r + 1) * LANES)
                units.append((q_ref[pl.ds(i0, Q_BLOCK), c], k_ref[pl.ds(ks, kw), c],
                              v_ref[pl.ds(ks, kw), c], bias))
                dests.append((blk, r))
        for (blk, r), res in zip(dests, _pair_attention(units, lo, False)):
            sink(blk, r, *res)

    def body(it, carry):
        for g0 in range(0, dil, gps):
            step(it * bps, g0)
        return carry

    if nblk == bps:
        body(0, 0)
    else:
        lax.fori_loop(0, nblk // bps, body, 0)


def _dilated_kernel(b1_ref, b4_ref, b16_ref, q1_ref, k1_ref, v1_ref, q4_ref, k4_ref, v4_ref,
                    q16_ref, k16_ref, v16_ref, out_ref, o_s, m_s, d_s):
    for slot, (dil, refs) in enumerate(((DILATIONS[1], (b4_ref, q4_ref, k4_ref, v4_ref)),
                                        (DILATIONS[2], (b16_ref, q16_ref, k16_ref, v16_ref)))):
        def scatter(blk, r, o, m, den, slot=slot, dil=dil):
            rows = pl.ds(blk * (Q_BLOCK * dil) + r, Q_BLOCK, stride=dil)
            o_s[slot, rows, :] = o
            m_s[slot, rows, :] = m
            d_s[slot, rows, :] = den
        _band_pattern(*refs, dil, scatter)

    def mix(blk, r, o1, m1, d1):
        i0 = blk * Q_BLOCK
        rows = pl.ds(i0 if isinstance(blk, int) else pl.multiple_of(i0, Q_BLOCK), Q_BLOCK)
        o2, m2, d2 = o_s[0, rows, :], m_s[0, rows, :], d_s[0, rows, :]
        o3, m3, d3 = o_s[1, rows, :], m_s[1, rows, :], d_s[1, rows, :]
        m = jnp.maximum(jnp.maximum(m1, m2), m3)
        e1, e2, e3 = jnp.exp2(m1 - m), jnp.exp2(m2 - m), jnp.exp2(m3 - m)
        out_ref[rows, :] = ((e1 * o1 + e2 * o2 + e3 * o3)
                            / (e1 * d1 + e2 * d2 + e3 * d3)).astype(BF16)

    _band_pattern(b1_ref, q1_ref, k1_ref, v1_ref, DILATIONS[0], mix)


def _band_bias(length):
    kw = min(2 * Q_BLOCK, length)
    qi = np.arange(Q_BLOCK)[:, None]
    kj = np.arange(kw)[None, :]
    offsets = (0,) if length == Q_BLOCK else (0, -BAND_HALF, Q_BLOCK - kw)
    tiles = [np.where(np.abs(kj + off - qi) <= BAND_HALF, 0.0, NEG_INF) for off in offsets]
    return jnp.asarray(np.tile(np.stack(tiles), (1, 2, 1)), F32)


def _dilated_attention(qkv_a, batch, seq):
    pairs = WIDTH // LANES
    biases = [_band_bias(seq // dil) for dil in DILATIONS]
    spec = lambda dil: pl.BlockSpec((None, None, seq // dil, dil * LANES), lambda b, g: (b, g, 0, 0))
    return pl.pallas_call(
        _dilated_kernel,
        grid=(batch, pairs),
        in_specs=[_const_spec(b.shape) for b in biases]
                 + [spec(dil) for dil in DILATIONS for _ in range(3)],
        out_specs=pl.BlockSpec((None, seq, LANES), lambda b, g: (b, 0, g)),
        out_shape=jax.ShapeDtypeStruct((batch, seq, WIDTH), BF16),
        scratch_shapes=[pltpu.VMEM((len(DILATIONS) - 1, seq, LANES), F32)] * 3,
        compiler_params=_params("parallel", "parallel"),
        name="dilated",
    )(*biases, *[a for qkv in qkv_a for a in qkv])


def _nbr_kernel(bias_ref, q_ref, k_ref, v_ref, o_ref, *, rows):
    lo = _lo_mask(GRID_W)
    nkeys = NA_ROWS * GRID_W

    pairs = WIDTH // LANES
    rps = UNITS_PER_STEP // pairs

    def step(it, carry):
        units, dests = [], []
        for j in range(rps):
            r = it * rps + j
            rs = jnp.clip(r - NA_ROWS // 2, 0, rows - NA_ROWS)
            var = r - rs
            q0 = pl.multiple_of(r * GRID_W, GRID_W)
            k0 = pl.multiple_of(rs * GRID_W, GRID_W)
            for p in range(pairs):
                c = slice(p * LANES, (p + 1) * LANES)
                units.append((q_ref[pl.ds(q0, GRID_W), c], k_ref[pl.ds(k0, nkeys), c],
                              v_ref[pl.ds(k0, nkeys), c], bias_ref[p * NA_ROWS + var]))
                dests.append((q0, c))
        for (q0, c), (o, _, _) in zip(dests, _pair_attention(units, lo, True)):
            o_ref[pl.ds(q0, GRID_W), c] = o.astype(BF16)
        return carry

    lax.fori_loop(0, rows // rps, step, 0)


def _nbr_bias(rpb):
    c = np.arange(GRID_W)[:, None]
    ck = np.arange(GRID_W)[None, :]
    cs = np.clip(c - NA_COLS // 2, 0, GRID_W - NA_COLS)
    ok = (ck >= cs) & (ck < cs + NA_COLS)
    pad = GRID_W - NA_COLS
    rp = jnp.pad(rpb.astype(F32) * LOG2E, ((0, 0), (0, 0), (pad, pad)))
    t = jnp.stack([rp[:, :, GRID_W - 1 - q:2 * GRID_W - 1 - q] for q in range(GRID_W)], axis=2)
    t = jnp.where(ok[None, None], t, NEG_INF)
    b = jnp.stack([t[:, NA_ROWS - 1 - v:2 * NA_ROWS - 1 - v] for v in range(NA_ROWS)], axis=1)
    b = b.transpose(0, 1, 3, 2, 4)
    b = b.reshape(N_HEADS // 2, 2, NA_ROWS, GRID_W, NA_ROWS * GRID_W).transpose(0, 2, 1, 3, 4)
    return b.reshape(N_HEADS // 2 * NA_ROWS, 2 * GRID_W, NA_ROWS * GRID_W)


def _nbr_attention(q, k, v, bias, batch, seq):
    view = lambda a: a.reshape(batch, seq, WIDTH)
    spec = pl.BlockSpec((None, seq, WIDTH), lambda b: (b, 0, 0))
    o = pl.pallas_call(
        functools.partial(_nbr_kernel, rows=seq // GRID_W),
        grid=(batch,),
        in_specs=[_const_spec(bias.shape), spec, spec, spec],
        out_specs=spec,
        out_shape=jax.ShapeDtypeStruct((batch, seq, WIDTH), BF16),
        compiler_params=_params("parallel"),
        name="nbr",
    )(bias, view(q), view(k), view(v))
    return o.reshape(batch * seq, WIDTH)


def _merge_kernel(oa_ref, ob_ref, ga_ref, gb_ref,
                  x_ref, mod_ref, g_ref, wa_ref, wb_ref, wo_ref, out_ref):
    ya = jnp.dot(oa_ref[...], wa_ref[...], preferred_element_type=F32)
    yb = jnp.dot(ob_ref[...], wb_ref[...], preferred_element_type=F32)
    merged = ga_ref[...].astype(F32) * ya + gb_ref[...].astype(F32) * yb
    z = jnp.dot(merged.astype(BF16), wo_ref[...], preferred_element_type=F32)
    out_ref[...] = x_ref[...] + mod_ref[2:3, :] * _rms(z, g_ref[...])


def _merge(oa, ob, ga, gb, x2, mod3, g_post, wa, wb, wo, seq, tm):
    t, d = x2.shape
    tps = seq // tm
    row = lambda w: pl.BlockSpec((tm, w), lambda i: (i, 0))
    return pl.pallas_call(
        _merge_kernel,
        grid=(t // tm,),
        in_specs=[row(WIDTH)] * 2 + [row(d)] * 3
                 + [pl.BlockSpec((None, 6, d), lambda i: (i // tps, 0, 0)),
                    _const_spec((1, d)), _const_spec(wa.shape), _const_spec(wb.shape),
                    _const_spec(wo.shape)],
        out_specs=row(d),
        out_shape=jax.ShapeDtypeStruct((t, d), F32),
        compiler_params=_params("parallel"),
        name="merge",
    )(oa, ob, ga, gb, x2, mod3, g_post, wa, wb, wo)


HALO = 16
FFN_TM = 512
SLABS = 8
FFN_CHUNKS = (256, 768, 768, 768, 256)


def _ffn_kernel(x_ref, xp_ref, xn_ref, mod_ref, gpre_ref, gpost_ref, wup_ref, cw_ref, cb_ref,
                wdn_ref, out_ref, uv_ref, ug_ref, xs_ref, os_ref, *, tps, chunks):
    tm = x_ref.shape[0]
    slab = tm // SLABS
    lane_groups = range(D_MODEL // LANES)
    i = pl.program_id(0)
    has_prev = (i % tps != 0).astype(F32)
    has_next = (i % tps != tps - 1).astype(F32)
    shift, gain, gpre = mod_ref[3:4, :], 1.0 + mod_ref[4:5, :], gpre_ref[...]

    def prenorm(x):
        return _rms(x, gpre) * gain + shift

    for g in lane_groups:
        xs_ref[g] = x_ref[:, g * LANES:(g + 1) * LANES]
    x = jnp.concatenate(
        [jnp.concatenate([xs_ref[g, pl.ds(j, slab, stride=SLABS), :] for g in lane_groups], axis=1)
         for j in range(SLABS)], axis=0)
    hext = jnp.concatenate([prenorm(x).astype(BF16),
                            (prenorm(xp_ref[...]) * has_prev).astype(BF16),
                            (prenorm(xn_ref[...]) * has_next).astype(BF16)], axis=0)
    prev_row, next_row = tm + HALO - 1, tm + HALO
    starts = [sum(chunks[:j]) for j in range(len(chunks))]

    def up(j):
        c0, w = starts[j], chunks[j]
        uv_ref[j % 2, :, 0:w] = jnp.dot(hext, wup_ref[:, c0:c0 + w], preferred_element_type=F32)
        ug_ref[j % 2, :, 0:w] = jnp.dot(hext, wup_ref[:, D_FF + c0:D_FF + c0 + w],
                                        preferred_element_type=F32)

    def activation(j):
        c0, w = starts[j], chunks[j]

        row = lax.broadcasted_iota(jnp.int32, (slab, w), 0)

        def conv(u_ref, col0):
            cols = slice(col0, col0 + w)
            w0, w1, w2, b = cw_ref[0:1, cols], cw_ref[1:2, cols], cw_ref[2:3, cols], cb_ref[:, cols]
            rows = lambda r0: u_ref[j % 2, pl.ds(r0, slab), 0:w]
            out = []
            for s in range(SLABS):
                prev = rows((s - 1) * slab) if s > 0 else jnp.where(
                    row == 0, u_ref[j % 2, prev_row:prev_row + 1, 0:w], rows((SLABS - 1) * slab - 1))
                nxt = rows((s + 1) * slab) if s < SLABS - 1 else jnp.where(
                    row == slab - 1, u_ref[j % 2, next_row:next_row + 1, 0:w], rows(1))
                out.append(w0 * prev + w1 * rows(s * slab) + w2 * nxt + b)
            return jnp.concatenate(out, axis=0)

        return (jax.nn.gelu(conv(ug_ref, D_FF + c0), approximate=True) * conv(uv_ref, c0)).astype(BF16)

    up(0)
    acc = jnp.zeros((tm, D_MODEL), F32)
    for j in range(len(chunks)):
        if j + 1 < len(chunks):
            up(j + 1)
        c0, w = starts[j], chunks[j]
        acc += jnp.dot(activation(j), wdn_ref[c0:c0 + w, :], preferred_element_type=F32)
    y = x + mod_ref[5:6, :] * _rms(acc, gpost_ref[...])
    for s in range(SLABS):
        for g in lane_groups:
            os_ref[g, pl.ds(s, slab, stride=SLABS), :] = y[s * slab:(s + 1) * slab,
                                                           g * LANES:(g + 1) * LANES]
    for g in lane_groups:
        out_ref[:, g * LANES:(g + 1) * LANES] = os_ref[g]


def _ffn(x1, mod3, g_pre, g_post, w_up, conv_w, conv_b, w_down, seq, tm, chunks):
    assert sum(chunks) == D_FF and all(c % LANES == 0 for c in chunks)
    t, d = x1.shape
    tps = seq // tm
    hpt = tm // HALO
    last = t // HALO - 1
    return pl.pallas_call(
        functools.partial(_ffn_kernel, tps=tps, chunks=chunks),
        grid=(t // tm,),
        in_specs=[pl.BlockSpec((tm, d), lambda i: (i, 0)),
                  pl.BlockSpec((HALO, d), lambda i: (jnp.maximum(i * hpt - 1, 0), 0)),
                  pl.BlockSpec((HALO, d), lambda i: (jnp.minimum((i + 1) * hpt, last), 0)),
                  pl.BlockSpec((None, 6, d), lambda i: (i // tps, 0, 0)),
                  _const_spec((1, d)), _const_spec((1, d)),
                  _const_spec(w_up.shape), _const_spec(conv_w.shape), _const_spec(conv_b.shape),
                  _const_spec(w_down.shape)],
        out_specs=pl.BlockSpec((tm, d), lambda i: (i, 0)),
        out_shape=jax.ShapeDtypeStruct((t, d), F32),
        scratch_shapes=[pltpu.VMEM((2, tm + 2 * HALO, max(chunks)), F32)] * 2
                       + [pltpu.VMEM((d // LANES, tm, LANES), F32)] * 2,
        compiler_params=_params("parallel"),
        name="ffn",
    )(x1, x1, x1, mod3, g_pre, g_post, w_up, conv_w, conv_b, w_down)


def _layer(x, c, p, tables, nbr_bias):
    batch, seq, d = x.shape
    x2 = x.reshape(batch * seq, d)
    mod3 = _modulation(c, p["w_ada"], p["b_ada"]).reshape(batch, 6, d)
    qkv_a, qkv_b, ga, gb = _inproj(x2, mod3, p["g_mix_pre"], p["w_in"], tables, seq, 512)
    oa = _dilated_attention(qkv_a, batch, seq).reshape(batch * seq, WIDTH)
    ob = _nbr_attention(*qkv_b, nbr_bias, batch, seq)
    x1 = _merge(oa, ob, ga, gb, x2, mod3,
                p["g_mix_post"], p["w_branch_a"], p["w_branch_b"], p["w_out"], seq, 512)
    y = _ffn(x1, mod3, p["g_ffn_pre"], p["g_ffn_post"], p["w_up"], p["conv_w"], p["conv_b"],
             p["w_down"], seq, FFN_TM, FFN_CHUNKS)
    return y.reshape(batch, seq, d)


def kernel(x_prompt, x_sample, c_prompt, c_sample, w_ada, b_ada, g_mix_pre, g_mix_post, g_ffn_pre,
           g_ffn_post, w_in, rpb, w_branch_a, w_branch_b, w_out, w_up, conv_w, conv_b, w_down):
    assert w_ada.shape[0] == 1, "single-layer trunk"
    row = lambda a: a[0].reshape(1, -1)
    p = {
        "w_ada": w_ada[0], "b_ada": b_ada[0],
        "g_mix_pre": row(g_mix_pre), "g_mix_post": row(g_mix_post),
        "g_ffn_pre": row(g_ffn_pre), "g_ffn_post": row(g_ffn_post),
        "w_in": w_in[0].astype(BF16),
        "w_branch_a": w_branch_a[0].astype(BF16), "w_branch_b": w_branch_b[0].astype(BF16),
        "w_out": w_out[0].astype(BF16), "w_up": w_up[0].astype(BF16),
        "conv_w": conv_w[0], "conv_b": row(conv_b), "w_down": w_down[0].astype(BF16),
    }
    nbr_bias = _nbr_bias(rpb[0])
    tables = _rope_tables(x_prompt.shape[1])
    y_prompt = _layer(x_prompt, c_prompt, p, tables, nbr_bias)
    y_sample = _layer(x_sample, c_sample, p, tables, nbr_bias)
    return (y_prompt, y_sample)
```

```python
import functools

import numpy as np
import jax
import jax.numpy as jnp
from jax import lax
from jax.experimental import pallas as pl
from jax.experimental.pallas import tpu as pltpu

D_MODEL = 1024
HEAD_DIM = 64
N_HEADS = 8
WIDTH = N_HEADS * HEAD_DIM
DILATIONS = (1, 4, 16)
BAND_HALF = 64
ROPE_THETA = 500000.0
ROPE_DIM = HEAD_DIM // 4
GRID_W = 64
NA_ROWS = 8
NA_COLS = 16
D_FF = 2816
EPS = 1e-6
NEG_INF = -1e30
LOG2E = 1.4426950408889634

LANES = 128
Q_BLOCK = 128
UNITS_PER_STEP = 16
VMEM_LIMIT = 56 * 1024 * 1024
INPROJ_TM = 512
MERGE_TM = 1024

F32 = jnp.float32
BF16 = jnp.bfloat16


def _params(*sem):
    return pltpu.CompilerParams(dimension_semantics=sem, vmem_limit_bytes=VMEM_LIMIT)


def _const_spec(shape):
    nd = len(shape)
    return pl.BlockSpec(shape, lambda *_: (0,) * nd, pipeline_mode=pl.Buffered(1))


def _rms(x, g):
    return x * lax.rsqrt(jnp.mean(x * x, axis=-1, keepdims=True) + EPS) * g


def _mod_kernel(c_ref, w_ref, b_ref, o_ref):
    c = c_ref[...]
    s = c * jax.nn.sigmoid(c)
    w = w_ref[...]
    s_hi = s.astype(BF16)
    s_lo = (s - s_hi.astype(F32)).astype(BF16)
    w_hi = w.astype(BF16)
    w_lo = (w - w_hi.astype(F32)).astype(BF16)
    acc = jnp.dot(s_hi, w_hi, preferred_element_type=F32)
    acc += jnp.dot(s_hi, w_lo, preferred_element_type=F32)
    acc += jnp.dot(s_lo, w_hi, preferred_element_type=F32)
    o_ref[...] = acc + b_ref[...]


def _modulation(c, w_ada, b_ada):
    b, d = c.shape
    n = w_ada.shape[1]
    return pl.pallas_call(
        _mod_kernel,
        grid=(n // d,),
        in_specs=[_const_spec((b, d)),
                  pl.BlockSpec((d, d), lambda j: (0, j)),
                  pl.BlockSpec((1, d), lambda j: (0, j))],
        out_specs=pl.BlockSpec((b, d), lambda j: (0, j)),
        out_shape=jax.ShapeDtypeStruct((b, n), F32),
        compiler_params=_params("arbitrary"),
        name="mod",
    )(c, w_ada, b_ada.reshape(1, n))


def _inproj_kernel(x_ref, mod_ref, g_ref, w_ref, cos_ref, sa_ref, sb_ref,
                   q1_ref, k1_ref, v1_ref, q4_ref, k4_ref, v4_ref, q16_ref, k16_ref, v16_ref,
                   qb_ref, kb_ref, vb_ref, ga_ref, gb_ref, stage_ref):
    tm = x_ref.shape[0]
    h = _rms(x_ref[...], g_ref[...]) * (1.0 + mod_ref[1:2, :]) + mod_ref[0:1, :]
    hb = h.astype(BF16)

    def proj(c0, width):
        return jnp.dot(hb, w_ref[:, c0:c0 + width], preferred_element_type=F32)

    cos, sa, sb = cos_ref[...], sa_ref[...], sb_ref[...]

    def rope(t):
        outs = []
        for g in range(t.shape[1] // LANES):
            tg = t[:, g * LANES:(g + 1) * LANES]
            outs.append(tg * cos + pltpu.roll(tg, ROPE_DIM // 2, 1) * sa
                        + pltpu.roll(tg, LANES - ROPE_DIM // 2, 1) * sb)
        return jnp.concatenate(outs, axis=1)

    def emit(t, refs):
        for g in range(WIDTH // LANES):
            tg = t[:, g * LANES:(g + 1) * LANES]
            refs[0][g] = tg.astype(BF16)
            stage_ref[g] = tg
        for dil, ref in zip(DILATIONS[1:], refs[1:]):
            for r in range(dil):
                for g in range(WIDTH // LANES):
                    ref[g, :, r * LANES:(r + 1) * LANES] = (
                        stage_ref[g, pl.ds(r, tm // dil, stride=dil), :].astype(BF16))

    scale = HEAD_DIM ** -0.5 * LOG2E

    emit(rope(proj(0, WIDTH)) * scale, (q1_ref, q4_ref, q16_ref))
    emit(rope(proj(WIDTH, WIDTH)), (k1_ref, k4_ref, k16_ref))
    emit(proj(2 * WIDTH, WIDTH), (v1_ref, v4_ref, v16_ref))
    qb_ref[...] = (proj(3 * WIDTH, WIDTH) * scale).astype(BF16)
    kb_ref[...] = proj(4 * WIDTH, WIDTH).astype(BF16)
    vb_ref[...] = proj(5 * WIDTH, WIDTH).astype(BF16)
    ga_ref[...] = jax.nn.sigmoid(proj(6 * WIDTH, D_MODEL)).astype(BF16)
    gb_ref[...] = jax.nn.sigmoid(proj(6 * WIDTH + D_MODEL, D_MODEL)).astype(BF16)


def _rope_tables(seq):
    half = ROPE_DIM // 2
    inv = jnp.power(jnp.float32(ROPE_THETA), -jnp.arange(half, dtype=F32) / half)
    ang = jnp.arange(seq).astype(F32)[:, None] * inv[None, :]
    cos, sin = jnp.cos(ang), jnp.sin(ang)
    ones = jnp.ones((seq, HEAD_DIM - ROPE_DIM), F32)
    zeros = jnp.zeros((seq, HEAD_DIM - ROPE_DIM), F32)
    zh = jnp.zeros((seq, half), F32)
    per_head = lambda *parts: jnp.tile(jnp.concatenate(parts, axis=1), (1, LANES // HEAD_DIM))
    return (per_head(cos, cos, ones),
            per_head(zh, sin, zeros),
            per_head(-sin, zh, zeros))


def _inproj(x2, mod3, g_pre, w_in, tables, seq, tm):
    t, d = x2.shape
    tps = seq // tm
    row = lambda w: pl.BlockSpec((tm, w), lambda i: (i, 0))
    tab = pl.BlockSpec((tm, LANES), lambda i: (i % tps, 0))
    pairs = WIDTH // LANES
    dil_shapes = [jax.ShapeDtypeStruct((t // seq, pairs, seq // dil, dil * LANES), BF16)
                  for dil in DILATIONS]
    dil_specs = [pl.BlockSpec((None, pairs, tm // dil, dil * LANES),
                              lambda i: (i // tps, 0, i % tps, 0)) for dil in DILATIONS]
    outs = pl.pallas_call(
        _inproj_kernel,
        grid=(t // tm,),
        in_specs=[row(d),
                  pl.BlockSpec((None, 6, d), lambda i: (i // tps, 0, 0)),
                  _const_spec((1, d)),
                  _const_spec(w_in.shape),
                  tab, tab, tab],
        out_specs=[s for s in dil_specs for _ in range(3)] + [row(WIDTH)] * 3 + [row(d)] * 2,
        out_shape=[s for s in dil_shapes for _ in range(3)]
                  + [jax.ShapeDtypeStruct((t, WIDTH), BF16)] * 3
                  + [jax.ShapeDtypeStruct((t, d), BF16)] * 2,
        scratch_shapes=[pltpu.VMEM((WIDTH // LANES, tm, LANES), F32)],
        compiler_params=_params("parallel"),
        name="inproj",
    )(x2, mod3, g_pre, w_in, *tables)
    qkv_a = [outs[3 * n:3 * n + 3] for n in range(len(DILATIONS))]
    return qkv_a, outs[9:12], outs[12], outs[13]


def _pair_attention(units, lo_mask, normalise):
    zero = jnp.zeros_like(units[0][0])
    scores = []
    for q2, kwin, _, bias in units:
        qx = jnp.concatenate([jnp.where(lo_mask, q2, zero), jnp.where(lo_mask, zero, q2)], axis=0)
        s = lax.dot_general(qx, kwin, (((1,), (1,)), ((), ())), preferred_element_type=F32)
        scores.append(s + bias)
    maxes = [jnp.max(s, axis=1, keepdims=True) for s in scores]
    probs = [jnp.exp2(s - m).astype(BF16) for s, m in zip(scores, maxes)]
    results = []
    for p, (_, _, vwin, _) in zip(probs, units):
        vaug = jnp.concatenate([vwin, jnp.ones((vwin.shape[0], LANES), BF16)], axis=1)
        results.append(jnp.dot(p, vaug, preferred_element_type=F32))
    outs = []
    for r, m, (q2, _, _, _) in zip(results, maxes, units):
        rows = q2.shape[0]
        o, den = r[:, :LANES], r[:, LANES:]
        if normalise:
            o = o / den
        pick = lambda t: jnp.where(lo_mask, t[:rows], t[rows:])
        outs.append((pick(o), pick(jnp.broadcast_to(m, den.shape)), pick(den)))
    return outs


def _lo_mask(rows):
    return lax.broadcasted_iota(jnp.int32, (rows, LANES), 1) < HEAD_DIM


def _band_pattern(bias_ref, q_ref, k_ref, v_ref, dil, sink):
    length = q_ref.shape[0]
    nblk = length // Q_BLOCK
    kw = bias_ref.shape[2]
    lo = _lo_mask(Q_BLOCK)
    gps = min(dil, UNITS_PER_STEP)
    bps = min(UNITS_PER_STEP // gps, nblk)

    def step(blk0, g0):
        units, dests = [], []
        for j in range(bps):
            blk = blk0 + j
            if isinstance(blk, int):
                i0 = blk * Q_BLOCK
                ks = min(max(i0 - BAND_HALF, 0), length - kw)
                kind = 0 if blk == 0 else (2 if blk == nblk - 1 else 1)
                kind = min(kind, bias_ref.shape[0] - 1)
            else:
                i0 = pl.multiple_of(blk * Q_BLOCK, Q_BLOCK)
                ks = pl.multiple_of(jnp.clip(i0 - BAND_HALF, 0, length - kw), BAND_HALF)
                kind = jnp.where(blk == 0, 0, jnp.where(blk == nblk - 1, 2, 1))
            bias = bias_ref[kind]
            for r in range(g0, g0 + gps):
                c = slice(r * LANES, (r + 1) * LANES)
                units.append((q_ref[pl.ds(i0, Q_BLOCK), c], k_ref[pl.ds(ks, kw), c],
                              v_ref[pl.ds(ks, kw), c], bias))
                dests.append((blk, r))
        for (blk, r), res in zip(dests, _pair_attention(units, lo, False)):
            sink(blk, r, *res)

    def body(it, carry):
        for g0 in range(0, dil, gps):
            step(it * bps, g0)
        return carry

    if nblk == bps:
        body(0, 0)
    else:
        lax.fori_loop(0, nblk // bps, body, 0)


def _dilated_kernel(b1_ref, b4_ref, b16_ref, q1_ref, k1_ref, v1_ref, q4_ref, k4_ref, v4_ref,
                    q16_ref, k16_ref, v16_ref, out_ref, o_s, m_s, d_s):
    for slot, (dil, refs) in enumerate(((DILATIONS[1], (b4_ref, q4_ref, k4_ref, v4_ref)),
                                        (DILATIONS[2], (b16_ref, q16_ref, k16_ref, v16_ref)))):
        def scatter(blk, r, o, m, den, slot=slot, dil=dil):
            rows = pl.ds(blk * (Q_BLOCK * dil) + r, Q_BLOCK, stride=dil)
            o_s[slot, rows, :] = o
            m_s[slot, rows, :] = m
            d_s[slot, rows, :] = den
        _band_pattern(*refs, dil, scatter)

    def mix(blk, r, o1, m1, d1):
        i0 = blk * Q_BLOCK
        rows = pl.ds(i0 if isinstance(blk, int) else pl.multiple_of(i0, Q_BLOCK), Q_BLOCK)
        o2, m2, d2 = o_s[0, rows, :], m_s[0, rows, :], d_s[0, rows, :]
        o3, m3, d3 = o_s[1, rows, :], m_s[1, rows, :], d_s[1, rows, :]
        m = jnp.maximum(jnp.maximum(m1, m2), m3)
        e1, e2, e3 = jnp.exp2(m1 - m), jnp.exp2(m2 - m), jnp.exp2(m3 - m)
        out_ref[rows, :] = ((e1 * o1 + e2 * o2 + e3 * o3)
                            / (e1 * d1 + e2 * d2 + e3 * d3)).astype(BF16)

    _band_pattern(b1_ref, q1_ref, k1_ref, v1_ref, DILATIONS[0], mix)


def _band_bias(length):
    kw = min(2 * Q_BLOCK, length)
    qi = np.arange(Q_BLOCK)[:, None]
    kj = np.arange(kw)[None, :]
    offsets = (0,) if length == Q_BLOCK else (0, -BAND_HALF, Q_BLOCK - kw)
    tiles = [np.where(np.abs(kj + off - qi) <= BAND_HALF, 0.0, NEG_INF) for off in offsets]
    return jnp.asarray(np.tile(np.stack(tiles), (1, 2, 1)), F32)


def _dilated_attention(qkv_a, batch, seq):
    pairs = WIDTH // LANES
    biases = [_band_bias(seq // dil) for dil in DILATIONS]
    spec = lambda dil: pl.BlockSpec((None, None, seq // dil, dil * LANES), lambda b, g: (b, g, 0, 0))
    return pl.pallas_call(
        _dilated_kernel,
        grid=(batch, pairs),
        in_specs=[_const_spec(b.shape) for b in biases]
                 + [spec(dil) for dil in DILATIONS for _ in range(3)],
        out_specs=pl.BlockSpec((None, seq, LANES), lambda b, g: (b, 0, g)),
        out_shape=jax.ShapeDtypeStruct((batch, seq, WIDTH), BF16),
        scratch_shapes=[pltpu.VMEM((len(DILATIONS) - 1, seq, LANES), F32)] * 3,
        compiler_params=_params("parallel", "parallel"),
        name="dilated",
    )(*biases, *[a for qkv in qkv_a for a in qkv])


def _nbr_kernel(bias_ref, q_ref, k_ref, v_ref, o_ref, *, rows):
    lo = _lo_mask(GRID_W)
    nkeys = NA_ROWS * GRID_W

    pairs = WIDTH // LANES
    rps = UNITS_PER_STEP // pairs

    def step(it, carry):
        units, dests = [], []
        for j in range(rps):
            r = it * rps + j
            rs = jnp.clip(r - NA_ROWS // 2, 0, rows - NA_ROWS)
            var = r - rs
            q0 = pl.multiple_of(r * GRID_W, GRID_W)
            k0 = pl.multiple_of(rs * GRID_W, GRID_W)
            for p in range(pairs):
                c = slice(p * LANES, (p + 1) * LANES)
                units.append((q_ref[pl.ds(q0, GRID_W), c], k_ref[pl.ds(k0, nkeys), c],
                              v_ref[pl.ds(k0, nkeys), c], bias_ref[p * NA_ROWS + var]))
                dests.append((q0, c))
        for (q0, c), (o, _, _) in zip(dests, _pair_attention(units, lo, True)):
            o_ref[pl.ds(q0, GRID_W), c] = o.astype(BF16)
        return carry

    lax.fori_loop(0, rows // rps, step, 0)


def _nbr_bias(rpb):
    c = np.arange(GRID_W)[:, None]
    ck = np.arange(GRID_W)[None, :]
    cs = np.clip(c - NA_COLS // 2, 0, GRID_W - NA_COLS)
    ok = (ck >= cs) & (ck < cs + NA_COLS)
    pad = GRID_W - NA_COLS
    rp = jnp.pad(rpb.astype(F32) * LOG2E, ((0, 0), (0, 0), (pad, pad)))
    t = jnp.stack([rp[:, :, GRID_W - 1 - q:2 * GRID_W - 1 - q] for q in range(GRID_W)], axis=1)
    t = jnp.where(ok[None, :, None, :], t, NEG_INF)
    t = t.reshape(N_HEADS // 2, 2 * GRID_W, 2 * NA_ROWS - 1, GRID_W)
    b = jnp.stack([t[:, :, NA_ROWS - 1 - v:2 * NA_ROWS - 1 - v] for v in range(NA_ROWS)], axis=1)
    return b.reshape(N_HEADS // 2 * NA_ROWS, 2 * GRID_W, NA_ROWS * GRID_W)


def _nbr_attention(q, k, v, bias, batch, seq):
    view = lambda a: a.reshape(batch, seq, WIDTH)
    spec = pl.BlockSpec((None, seq, WIDTH), lambda b: (b, 0, 0))
    o = pl.pallas_call(
        functools.partial(_nbr_kernel, rows=seq // GRID_W),
        grid=(batch,),
        in_specs=[_const_spec(bias.shape), spec, spec, spec],
        out_specs=spec,
        out_shape=jax.ShapeDtypeStruct((batch, seq, WIDTH), BF16),
        compiler_params=_params("parallel"),
        name="nbr",
    )(bias, view(q), view(k), view(v))
    return o.reshape(batch * seq, WIDTH)


def _merge_kernel(oa_ref, ob_ref, ga_ref, gb_ref,
                  x_ref, mod_ref, g_ref, wa_ref, wb_ref, wo_ref, out_ref):
    ya = jnp.dot(oa_ref[...], wa_ref[...], preferred_element_type=F32)
    yb = jnp.dot(ob_ref[...], wb_ref[...], preferred_element_type=F32)
    merged = ga_ref[...].astype(F32) * ya + gb_ref[...].astype(F32) * yb
    z = jnp.dot(merged.astype(BF16), wo_ref[...], preferred_element_type=F32)
    out_ref[...] = x_ref[...] + mod_ref[2:3, :] * _rms(z, g_ref[...])


def _merge(oa, ob, ga, gb, x2, mod3, g_post, wa, wb, wo, seq, tm):
    t, d = x2.shape
    tps = seq // tm
    row = lambda w: pl.BlockSpec((tm, w), lambda i: (i, 0))
    return pl.pallas_call(
        _merge_kernel,
        grid=(t // tm,),
        in_specs=[row(WIDTH)] * 2 + [row(d)] * 3
                 + [pl.BlockSpec((None, 6, d), lambda i: (i // tps, 0, 0)),
                    _const_spec((1, d)), _const_spec(wa.shape), _const_spec(wb.shape),
                    _const_spec(wo.shape)],
        out_specs=row(d),
        out_shape=jax.ShapeDtypeStruct((t, d), F32),
        compiler_params=_params("parallel"),
        name="merge",
    )(oa, ob, ga, gb, x2, mod3, g_post, wa, wb, wo)


HALO = 16
FFN_TM = 512
SLABS = 8
FFN_CHUNKS = (256, 768, 768, 768, 256)


def _ffn_kernel(x_ref, xp_ref, xn_ref, mod_ref, gpre_ref, gpost_ref, wup_ref, cw_ref, cb_ref,
                wdn_ref, out_ref, uv_ref, ug_ref, xs_ref, os_ref, *, tps, chunks):
    tm = x_ref.shape[0]
    slab = tm // SLABS
    lane_groups = range(D_MODEL // LANES)
    i = pl.program_id(0)
    has_prev = (i % tps != 0).astype(F32)
    has_next = (i % tps != tps - 1).astype(F32)
    shift, gain, gpre = mod_ref[3:4, :], 1.0 + mod_ref[4:5, :], gpre_ref[...]

    def prenorm(x):
        return _rms(x, gpre) * gain + shift

    for g in lane_groups:
        xs_ref[g] = x_ref[:, g * LANES:(g + 1) * LANES]
    x = jnp.concatenate(
        [jnp.concatenate([xs_ref[g, pl.ds(j, slab, stride=SLABS), :] for g in lane_groups], axis=1)
         for j in range(SLABS)], axis=0)
    hext = jnp.concatenate([prenorm(x).astype(BF16),
                            (prenorm(xp_ref[...]) * has_prev).astype(BF16),
                            (prenorm(xn_ref[...]) * has_next).astype(BF16)], axis=0)
    prev_row, next_row = tm + HALO - 1, tm + HALO
    starts = [sum(chunks[:j]) for j in range(len(chunks))]

    def up(j):
        c0, w = starts[j], chunks[j]
        uv_ref[j % 2, :, 0:w] = jnp.dot(hext, wup_ref[:, c0:c0 + w], preferred_element_type=F32)
        ug_ref[j % 2, :, 0:w] = jnp.dot(hext, wup_ref[:, D_FF + c0:D_FF + c0 + w],
                                        preferred_element_type=F32)

    def activation(j):
        c0, w = starts[j], chunks[j]

        row = lax.broadcasted_iota(jnp.int32, (slab, w), 0)

        def conv(u_ref, col0):
            cols = slice(col0, col0 + w)
            w0, w1, w2, b = cw_ref[0:1, cols], cw_ref[1:2, cols], cw_ref[2:3, cols], cb_ref[:, cols]
            rows = lambda r0: u_ref[j % 2, pl.ds(r0, slab), 0:w]
            out = []
            for s in range(SLABS):
                prev = rows((s - 1) * slab) if s > 0 else jnp.where(
                    row == 0, u_ref[j % 2, prev_row:prev_row + 1, 0:w], rows((SLABS - 1) * slab - 1))
                nxt = rows((s + 1) * slab) if s < SLABS - 1 else jnp.where(
                    row == slab - 1, u_ref[j % 2, next_row:next_row + 1, 0:w], rows(1))
                out.append(w0 * prev + w1 * rows(s * slab) + w2 * nxt + b)
            return jnp.concatenate(out, axis=0)

        return (jax.nn.gelu(conv(ug_ref, D_FF + c0), approximate=True) * conv(uv_ref, c0)).astype(BF16)

    up(0)
    acc = jnp.zeros((tm, D_MODEL), F32)
    for j in range(len(chunks)):
        if j + 1 < len(chunks):
            up(j + 1)
        c0, w = starts[j], chunks[j]
        acc += jnp.dot(activation(j), wdn_ref[c0:c0 + w, :], preferred_element_type=F32)
    y = x + mod_ref[5:6, :] * _rms(acc, gpost_ref[...])
    for s in range(SLABS):
        for g in lane_groups:
            os_ref[g, pl.ds(s, slab, stride=SLABS), :] = y[s * slab:(s + 1) * slab,
                                                           g * LANES:(g + 1) * LANES]
    for g in lane_groups:
        out_ref[:, g * LANES:(g + 1) * LANES] = os_ref[g]


def _ffn(x1, mod3, g_pre, g_post, w_up, conv_w, conv_b, w_down, seq, tm, chunks):
    assert sum(chunks) == D_FF and all(c % LANES == 0 for c in chunks)
    t, d = x1.shape
    tps = seq // tm
    hpt = tm // HALO
    last = t // HALO - 1
    return pl.pallas_call(
        functools.partial(_ffn_kernel, tps=tps, chunks=chunks),
        grid=(t // tm,),
        in_specs=[pl.BlockSpec((tm, d), lambda i: (i, 0)),
                  pl.BlockSpec((HALO, d), lambda i: (jnp.maximum(i * hpt - 1, 0), 0)),
                  pl.BlockSpec((HALO, d), lambda i: (jnp.minimum((i + 1) * hpt, last), 0)),
                  pl.BlockSpec((None, 6, d), lambda i: (i // tps, 0, 0)),
                  _const_spec((1, d)), _const_spec((1, d)),
                  _const_spec(w_up.shape), _const_spec(conv_w.shape), _const_spec(conv_b.shape),
                  _const_spec(w_down.shape)],
        out_specs=pl.BlockSpec((tm, d), lambda i: (i, 0)),
        out_shape=jax.ShapeDtypeStruct((t, d), F32),
        scratch_shapes=[pltpu.VMEM((2, tm + 2 * HALO, max(chunks)), F32)] * 2
                       + [pltpu.VMEM((d // LANES, tm, LANES), F32)] * 2,
        compiler_params=_params("parallel"),
        name="ffn",
    )(x1, x1, x1, mod3, g_pre, g_post, w_up, conv_w, conv_b, w_down)


def _layer(x, mod3, p, tables, nbr_bias):
    batch, seq, d = x.shape
    x2 = x.reshape(batch * seq, d)
    qkv_a, qkv_b, ga, gb = _inproj(x2, mod3, p["g_mix_pre"], p["w_in"], tables, seq, INPROJ_TM)
    oa = _dilated_attention(qkv_a, batch, seq).reshape(batch * seq, WIDTH)
    ob = _nbr_attention(*qkv_b, nbr_bias, batch, seq)
    x1 = _merge(oa, ob, ga, gb, x2, mod3,
                p["g_mix_post"], p["w_branch_a"], p["w_branch_b"], p["w_out"], seq, MERGE_TM)
    y = _ffn(x1, mod3, p["g_ffn_pre"], p["g_ffn_post"], p["w_up"], p["conv_w"], p["conv_b"],
             p["w_down"], seq, FFN_TM, FFN_CHUNKS)
    return y.reshape(batch, seq, d)


def kernel(x_prompt, x_sample, c_prompt, c_sample, w_ada, b_ada, g_mix_pre, g_mix_post, g_ffn_pre,
           g_ffn_post, w_in, rpb, w_branch_a, w_branch_b, w_out, w_up, conv_w, conv_b, w_down):
    assert w_ada.shape[0] == 1, "single-layer trunk"
    row = lambda a: a[0].reshape(1, -1)
    p = {
        "w_ada": w_ada[0], "b_ada": b_ada[0],
        "g_mix_pre": row(g_mix_pre), "g_mix_post": row(g_mix_post),
        "g_ffn_pre": row(g_ffn_pre), "g_ffn_post": row(g_ffn_post),
        "w_in": w_in[0].astype(BF16),
        "w_branch_a": w_branch_a[0].astype(BF16), "w_branch_b": w_branch_b[0].astype(BF16),
        "w_out": w_out[0].astype(BF16), "w_up": w_up[0].astype(BF16),
        "conv_w": conv_w[0], "conv_b": row(conv_b), "w_down": w_down[0].astype(BF16),
    }
    nbr_bias = _nbr_bias(rpb[0])
    tables = _rope_tables(x_prompt.shape[1])
    n_prompt = x_prompt.shape[0]
    mod = _modulation(jnp.concatenate([c_prompt, c_sample], axis=0), p["w_ada"], p["b_ada"])
    mod3 = mod.reshape(mod.shape[0], 6, D_MODEL)
    y_prompt = _layer(x_prompt, mod3[:n_prompt], p, tables, nbr_bias)
    y_sample = _layer(x_sample, mod3[n_prompt:], p, tables, nbr_bias)
    return (y_prompt, y_sample)
```

```python
import functools

import numpy as np
import jax
import jax.numpy as jnp
from jax import lax
from jax.experimental import pallas as pl
from jax.experimental.pallas import tpu as pltpu

D_MODEL = 1024
HEAD_DIM = 64
N_HEADS = 8
WIDTH = N_HEADS * HEAD_DIM
DILATIONS = (1, 4, 16)
BAND_HALF = 64
ROPE_THETA = 500000.0
ROPE_DIM = HEAD_DIM // 4
GRID_W = 64
NA_ROWS = 8
NA_COLS = 16
D_FF = 2816
EPS = 1e-6
NEG_INF = -1e30
LOG2E = 1.4426950408889634
GELU_C = 0.7978845608028654

LANES = 128
Q_BLOCK = 128
NBR_ROWS_PER_STEP = 8
VMEM_LIMIT = 56 * 1024 * 1024
INPROJ_TM = 512
MERGE_TM = 1024

F32 = jnp.float32
BF16 = jnp.bfloat16


def _params(*sem):
    return pltpu.CompilerParams(dimension_semantics=sem, vmem_limit_bytes=VMEM_LIMIT)


def _const_spec(shape):
    nd = len(shape)
    return pl.BlockSpec(shape, lambda *_: (0,) * nd, pipeline_mode=pl.Buffered(1))


def _rms(x, g):
    return x * lax.rsqrt(jnp.mean(x * x, axis=-1, keepdims=True) + EPS) * g


def _mod_kernel(c_ref, w_ref, b_ref, o_ref):
    c = c_ref[...]
    s = c * jax.nn.sigmoid(c)
    w = w_ref[...]
    s_hi = s.astype(BF16)
    s_lo = (s - s_hi.astype(F32)).astype(BF16)
    w_hi = w.astype(BF16)
    w_lo = (w - w_hi.astype(F32)).astype(BF16)
    acc = jnp.dot(s_hi, w_hi, preferred_element_type=F32)
    acc += jnp.dot(s_hi, w_lo, preferred_element_type=F32)
    acc += jnp.dot(s_lo, w_hi, preferred_element_type=F32)
    o_ref[...] = acc + b_ref[...]


def _modulation(c, w_ada, b_ada):
    b, d = c.shape
    n = w_ada.shape[1]
    return pl.pallas_call(
        _mod_kernel,
        grid=(n // d,),
        in_specs=[_const_spec((b, d)),
                  pl.BlockSpec((d, d), lambda j: (0, j)),
                  pl.BlockSpec((1, d), lambda j: (0, j))],
        out_specs=pl.BlockSpec((b, d), lambda j: (0, j)),
        out_shape=jax.ShapeDtypeStruct((b, n), F32),
        compiler_params=_params("arbitrary"),
        name="mod",
    )(c, w_ada, b_ada.reshape(1, n))


def _inproj_kernel(x_ref, mod_ref, g_ref, w_ref, cos_ref, sa_ref, sb_ref,
                   q1_ref, k1_ref, v1_ref, q4_ref, k4_ref, v4_ref, q16_ref, k16_ref, v16_ref,
                   qb_ref, kb_ref, vb_ref, ga_ref, gb_ref, stage_ref):
    tm = x_ref.shape[0]
    h = _rms(x_ref[...], g_ref[...]) * (1.0 + mod_ref[1:2, :]) + mod_ref[0:1, :]
    hb = h.astype(BF16)

    def proj(c0, width):
        return jnp.dot(hb, w_ref[:, c0:c0 + width], preferred_element_type=F32)

    cos, sa, sb = cos_ref[...], sa_ref[...], sb_ref[...]

    def rope(t):
        outs = []
        for g in range(t.shape[1] // LANES):
            tg = t[:, g * LANES:(g + 1) * LANES]
            outs.append(tg * cos + pltpu.roll(tg, ROPE_DIM // 2, 1) * sa
                        + pltpu.roll(tg, LANES - ROPE_DIM // 2, 1) * sb)
        return jnp.concatenate(outs, axis=1)

    def emit(t, refs):
        for g in range(WIDTH // LANES):
            tg = t[:, g * LANES:(g + 1) * LANES]
            refs[0][g] = tg.astype(BF16)
            stage_ref[g] = tg
        for dil, ref in zip(DILATIONS[1:], refs[1:]):
            for r in range(dil):
                for g in range(WIDTH // LANES):
                    ref[g, :, r * LANES:(r + 1) * LANES] = (
                        stage_ref[g, pl.ds(r, tm // dil, stride=dil), :].astype(BF16))

    scale = HEAD_DIM ** -0.5 * LOG2E

    emit(rope(proj(0, WIDTH)) * scale, (q1_ref, q4_ref, q16_ref))
    emit(rope(proj(WIDTH, WIDTH)), (k1_ref, k4_ref, k16_ref))
    emit(proj(2 * WIDTH, WIDTH), (v1_ref, v4_ref, v16_ref))
    qb_ref[...] = (proj(3 * WIDTH, WIDTH) * scale).astype(BF16)
    kb_ref[...] = proj(4 * WIDTH, WIDTH).astype(BF16)
    vb_ref[...] = proj(5 * WIDTH, WIDTH).astype(BF16)
    ga_ref[...] = jax.nn.sigmoid(proj(6 * WIDTH, D_MODEL)).astype(BF16)
    gb_ref[...] = jax.nn.sigmoid(proj(6 * WIDTH + D_MODEL, D_MODEL)).astype(BF16)


def _rope_tables(seq):
    half = ROPE_DIM // 2
    inv = jnp.power(jnp.float32(ROPE_THETA), -jnp.arange(half, dtype=F32) / half)
    ang = jnp.arange(seq).astype(F32)[:, None] * inv[None, :]
    cos, sin = jnp.cos(ang), jnp.sin(ang)
    ones = jnp.ones((seq, HEAD_DIM - ROPE_DIM), F32)
    zeros = jnp.zeros((seq, HEAD_DIM - ROPE_DIM), F32)
    zh = jnp.zeros((seq, half), F32)
    per_head = lambda *parts: jnp.tile(jnp.concatenate(parts, axis=1), (1, LANES // HEAD_DIM))
    return (per_head(cos, cos, ones),
            per_head(zh, sin, zeros),
            per_head(-sin, zh, zeros))


def _inproj(x2, mod3, g_pre, w_in, tables, seq, tm):
    t, d = x2.shape
    tps = seq // tm
    row = lambda w: pl.BlockSpec((tm, w), lambda i: (i, 0))
    tab = pl.BlockSpec((tm, LANES), lambda i: (i % tps, 0))
    pairs = WIDTH // LANES
    dil_shapes = [jax.ShapeDtypeStruct((t // seq, pairs, seq // dil, dil * LANES), BF16)
                  for dil in DILATIONS]
    dil_specs = [pl.BlockSpec((None, pairs, tm // dil, dil * LANES),
                              lambda i: (i // tps, 0, i % tps, 0)) for dil in DILATIONS]
    outs = pl.pallas_call(
        _inproj_kernel,
        grid=(t // tm,),
        in_specs=[row(d),
                  pl.BlockSpec((None, 6, d), lambda i: (i // tps, 0, 0)),
                  _const_spec((1, d)),
                  _const_spec(w_in.shape),
                  tab, tab, tab],
        out_specs=[s for s in dil_specs for _ in range(3)] + [row(WIDTH)] * 3 + [row(d)] * 2,
        out_shape=[s for s in dil_shapes for _ in range(3)]
                  + [jax.ShapeDtypeStruct((t, WIDTH), BF16)] * 3
                  + [jax.ShapeDtypeStruct((t, d), BF16)] * 2,
        scratch_shapes=[pltpu.VMEM((WIDTH // LANES, tm, LANES), F32)],
        compiler_params=_params("parallel"),
        name="inproj",
    )(x2, mod3, g_pre, w_in, *tables)
    qkv_a = [outs[3 * n:3 * n + 3] for n in range(len(DILATIONS))]
    return qkv_a, outs[9:12], outs[12], outs[13]


def _pair_attention(units, lo_mask, normalise):
    zero = jnp.zeros_like(units[0][0])
    scores = []
    for q2, kwin, _, bias in units:
        qx = jnp.concatenate([jnp.where(lo_mask, q2, zero), jnp.where(lo_mask, zero, q2)], axis=0)
        s = lax.dot_general(qx, kwin, (((1,), (1,)), ((), ())), preferred_element_type=F32)
        scores.append(s + bias)
    maxes = [jnp.max(s, axis=1, keepdims=True) for s in scores]
    probs = [jnp.exp2(s - m).astype(BF16) for s, m in zip(scores, maxes)]
    results = []
    for p, (_, _, vwin, _) in zip(probs, units):
        vaug = jnp.concatenate([vwin, jnp.ones((vwin.shape[0], LANES), BF16)], axis=1)
        results.append(jnp.dot(p, vaug, preferred_element_type=F32))
    outs = []
    for r, m, (q2, _, _, _) in zip(results, maxes, units):
        rows = q2.shape[0]
        o, den = r[:, :LANES], r[:, LANES:]
        if normalise:
            o = o / den
        pick = lambda t: jnp.where(lo_mask, t[:rows], t[rows:])
        outs.append((pick(o), pick(jnp.broadcast_to(m, den.shape)), pick(den)))
    return outs


def _lo_mask(rows):
    return lax.broadcasted_iota(jnp.int32, (rows, LANES), 1) < HEAD_DIM


def _band_units(bias_ref, q_ref, k_ref, v_ref, dil):
    length = q_ref.shape[0]
    nblk = length // Q_BLOCK
    kw = bias_ref.shape[2]
    units, dests = [], []
    for blk in range(nblk):
        i0 = blk * Q_BLOCK
        ks = min(max(i0 - BAND_HALF, 0), length - kw)
        kind = 0 if blk == 0 else (2 if blk == nblk - 1 else 1)
        bias = bias_ref[min(kind, bias_ref.shape[0] - 1)]
        for r in range(dil):
            c = slice(r * LANES, (r + 1) * LANES)
            units.append((q_ref[i0:i0 + Q_BLOCK, c], k_ref[ks:ks + kw, c], v_ref[ks:ks + kw, c], bias))
            dests.append((blk, r))
    return units, dests


def _dilated_kernel(b1_ref, b4_ref, b16_ref, q1_ref, k1_ref, v1_ref, q4_ref, k4_ref, v4_ref,
                    q16_ref, k16_ref, v16_ref, out_ref, o_s, m_s, d_s):
    lo = _lo_mask(Q_BLOCK)
    for slot, (dil, refs) in enumerate(((DILATIONS[1], (b4_ref, q4_ref, k4_ref, v4_ref)),
                                        (DILATIONS[2], (b16_ref, q16_ref, k16_ref, v16_ref)))):
        units, dests = _band_units(*refs, dil)
        for (blk, r), (o, m, den) in zip(dests, _pair_attention(units, lo, False)):
            rows = pl.ds(blk * (Q_BLOCK * dil) + r, Q_BLOCK, stride=dil)
            o_s[slot, rows, :] = o
            m_s[slot, rows, :] = m
            d_s[slot, rows, :] = den

    units, dests = _band_units(b1_ref, q1_ref, k1_ref, v1_ref, DILATIONS[0])
    for (blk, _), (o1, m1, d1) in zip(dests, _pair_attention(units, lo, False)):
        rows = slice(blk * Q_BLOCK, (blk + 1) * Q_BLOCK)
        o2, m2, d2 = o_s[0, rows, :], m_s[0, rows, :], d_s[0, rows, :]
        o3, m3, d3 = o_s[1, rows, :], m_s[1, rows, :], d_s[1, rows, :]
        m = jnp.maximum(jnp.maximum(m1, m2), m3)
        e1, e2, e3 = jnp.exp2(m1 - m), jnp.exp2(m2 - m), jnp.exp2(m3 - m)
        out_ref[rows, :] = ((e1 * o1 + e2 * o2 + e3 * o3)
                            / (e1 * d1 + e2 * d2 + e3 * d3)).astype(BF16)


def _band_bias(length):
    kw = min(2 * Q_BLOCK, length)
    qi = np.arange(Q_BLOCK)[:, None]
    kj = np.arange(kw)[None, :]
    offsets = (0,) if length == Q_BLOCK else (0, -BAND_HALF, Q_BLOCK - kw)
    tiles = [np.where(np.abs(kj + off - qi) <= BAND_HALF, 0.0, NEG_INF) for off in offsets]
    return jnp.asarray(np.tile(np.stack(tiles), (1, 2, 1)), F32)


def _dilated_attention(qkv_a, batch, seq):
    pairs = WIDTH // LANES
    biases = [_band_bias(seq // dil) for dil in DILATIONS]
    spec = lambda dil: pl.BlockSpec((None, None, seq // dil, dil * LANES), lambda b, g: (b, g, 0, 0))
    return pl.pallas_call(
        _dilated_kernel,
        grid=(batch, pairs),
        in_specs=[_const_spec(b.shape) for b in biases]
                 + [spec(dil) for dil in DILATIONS for _ in range(3)],
        out_specs=pl.BlockSpec((None, seq, LANES), lambda b, g: (b, 0, g)),
        out_shape=jax.ShapeDtypeStruct((batch, seq, WIDTH), BF16),
        scratch_shapes=[pltpu.VMEM((len(DILATIONS) - 1, seq, LANES), F32)] * 3,
        compiler_params=_params("parallel", "parallel"),
        name="dilated",
    )(*biases, *[a for qkv in qkv_a for a in qkv])


def _nbr_kernel(bias_ref, q_ref, k_ref, v_ref, o_ref, *, rows):
    lo = _lo_mask(GRID_W)
    nkeys = NA_ROWS * GRID_W

    pairs = WIDTH // LANES
    rps = NBR_ROWS_PER_STEP

    def step(it, carry):
        units, dests = [], []
        for j in range(rps):
            r = it * rps + j
            rs = jnp.clip(r - NA_ROWS // 2, 0, rows - NA_ROWS)
            var = r - rs
            q0 = pl.multiple_of(r * GRID_W, GRID_W)
            k0 = pl.multiple_of(rs * GRID_W, GRID_W)
            for p in range(pairs):
                c = slice(p * LANES, (p + 1) * LANES)
                units.append((q_ref[pl.ds(q0, GRID_W), c], k_ref[pl.ds(k0, nkeys), c],
                              v_ref[pl.ds(k0, nkeys), c], bias_ref[p * NA_ROWS + var]))
                dests.append((q0, c))
        for (q0, c), (o, _, _) in zip(dests, _pair_attention(units, lo, True)):
            o_ref[pl.ds(q0, GRID_W), c] = o.astype(BF16)
        return carry

    lax.fori_loop(0, rows // rps, step, 0)


def _nbr_bias(rpb):
    c = np.arange(GRID_W)[:, None]
    ck = np.arange(GRID_W)[None, :]
    cs = np.clip(c - NA_COLS // 2, 0, GRID_W - NA_COLS)
    ok = (ck >= cs) & (ck < cs + NA_COLS)
    pad = GRID_W - NA_COLS
    rp = jnp.pad(rpb.astype(F32) * LOG2E, ((0, 0), (0, 0), (pad, pad)))
    t = jnp.stack([rp[:, :, GRID_W - 1 - q:2 * GRID_W - 1 - q] for q in range(GRID_W)], axis=1)
    t = jnp.where(ok[None, :, None, :], t, NEG_INF)
    t = t.reshape(N_HEADS // 2, 2 * GRID_W, 2 * NA_ROWS - 1, GRID_W)
    b = jnp.stack([t[:, :, NA_ROWS - 1 - v:2 * NA_ROWS - 1 - v] for v in range(NA_ROWS)], axis=1)
    return b.reshape(N_HEADS // 2 * NA_ROWS, 2 * GRID_W, NA_ROWS * GRID_W)


def _nbr_attention(q, k, v, bias, batch, seq):
    view = lambda a: a.reshape(batch, seq, WIDTH)
    spec = pl.BlockSpec((None, seq, WIDTH), lambda b: (b, 0, 0))
    o = pl.pallas_call(
        functools.partial(_nbr_kernel, rows=seq // GRID_W),
        grid=(batch,),
        in_specs=[_const_spec(bias.shape), spec, spec, spec],
        out_specs=spec,
        out_shape=jax.ShapeDtypeStruct((batch, seq, WIDTH), BF16),
        compiler_params=_params("parallel"),
        name="nbr",
    )(bias, view(q), view(k), view(v))
    return o.reshape(batch * seq, WIDTH)


def _merge_kernel(oa_ref, ob_ref, ga_ref, gb_ref,
                  x_ref, mod_ref, g_ref, wa_ref, wb_ref, wo_ref, out_ref):
    ya = jnp.dot(oa_ref[...], wa_ref[...], preferred_element_type=F32)
    yb = jnp.dot(ob_ref[...], wb_ref[...], preferred_element_type=F32)
    merged = ga_ref[...].astype(F32) * ya + gb_ref[...].astype(F32) * yb
    z = jnp.dot(merged.astype(BF16), wo_ref[...], preferred_element_type=F32)
    out_ref[...] = x_ref[...] + mod_ref[2:3, :] * _rms(z, g_ref[...])


def _merge(oa, ob, ga, gb, x2, mod3, g_post, wa, wb, wo, seq, tm):
    t, d = x2.shape
    tps = seq // tm
    row = lambda w: pl.BlockSpec((tm, w), lambda i: (i, 0))
    return pl.pallas_call(
        _merge_kernel,
        grid=(t // tm,),
        in_specs=[row(WIDTH)] * 2 + [row(d)] * 3
                 + [pl.BlockSpec((None, 6, d), lambda i: (i // tps, 0, 0)),
                    _const_spec((1, d)), _const_spec(wa.shape), _const_spec(wb.shape),
                    _const_spec(wo.shape)],
        out_specs=row(d),
        out_shape=jax.ShapeDtypeStruct((t, d), F32),
        compiler_params=_params("parallel"),
        name="merge",
    )(oa, ob, ga, gb, x2, mod3, g_post, wa, wb, wo)


HALO = 16
FFN_TM = 512
SLABS = 8
FFN_CHUNKS = (256, 768, 768, 768, 256)


def _ffn_kernel(x_ref, xp_ref, xn_ref, mod_ref, gpre_ref, gpost_ref, wup_ref, cw_ref, cb_ref,
                wdn_ref, out_ref, uv_ref, ug_ref, xs_ref, os_ref, *, tps, chunks):
    tm = x_ref.shape[0]
    slab = tm // SLABS
    lane_groups = range(D_MODEL // LANES)
    i = pl.program_id(0)
    has_prev = (i % tps != 0).astype(F32)
    has_next = (i % tps != tps - 1).astype(F32)
    shift, gain, gpre = mod_ref[3:4, :], 1.0 + mod_ref[4:5, :], gpre_ref[...]

    def prenorm(x):
        return _rms(x, gpre) * gain + shift

    for g in lane_groups:
        xs_ref[g] = x_ref[:, g * LANES:(g + 1) * LANES]
    x = jnp.concatenate(
        [jnp.concatenate([xs_ref[g, pl.ds(j, slab, stride=SLABS), :] for g in lane_groups], axis=1)
         for j in range(SLABS)], axis=0)
    hext = jnp.concatenate([prenorm(x).astype(BF16),
                            (prenorm(xp_ref[...]) * has_prev).astype(BF16),
                            (prenorm(xn_ref[...]) * has_next).astype(BF16)], axis=0)
    prev_row, next_row = tm + HALO - 1, tm + HALO
    starts = [sum(chunks[:j]) for j in range(len(chunks))]

    def up(j):
        c0, w = starts[j], chunks[j]
        uv_ref[j % 2, :, 0:w] = jnp.dot(hext, wup_ref[:, c0:c0 + w], preferred_element_type=F32)
        ug_ref[j % 2, :, 0:w] = jnp.dot(hext, wup_ref[:, D_FF + c0:D_FF + c0 + w],
                                        preferred_element_type=F32)

    def activation(j):
        c0, w = starts[j], chunks[j]

        row = lax.broadcasted_iota(jnp.int32, (slab, w), 0)

        def conv(u_ref, col0, gain):
            cols = slice(col0, col0 + w)
            w0, w1, w2, b = (gain * t for t in (cw_ref[0:1, cols], cw_ref[1:2, cols],
                                                cw_ref[2:3, cols], cb_ref[:, cols]))
            rows = lambda r0: u_ref[j % 2, pl.ds(r0, slab), 0:w]
            out = []
            for s in range(SLABS):
                prev = rows((s - 1) * slab) if s > 0 else jnp.where(
                    row == 0, u_ref[j % 2, prev_row:prev_row + 1, 0:w], rows((SLABS - 1) * slab - 1))
                nxt = rows((s + 1) * slab) if s < SLABS - 1 else jnp.where(
                    row == slab - 1, u_ref[j % 2, next_row:next_row + 1, 0:w], rows(1))
                out.append(w0 * prev + w1 * rows(s * slab) + w2 * nxt + b)
            return jnp.concatenate(out, axis=0)

        g = conv(ug_ref, D_FF + c0, 1.0)
        gv = g * conv(uv_ref, c0, 0.5)
        t = jnp.tanh(g * (GELU_C + (GELU_C * 0.044715) * (g * g)))
        return (gv + gv * t).astype(BF16)

    up(0)
    acc = jnp.zeros((tm, D_MODEL), F32)
    for j in range(len(chunks)):
        if j + 1 < len(chunks):
            up(j + 1)
        c0, w = starts[j], chunks[j]
        acc += jnp.dot(activation(j), wdn_ref[c0:c0 + w, :], preferred_element_type=F32)
    y = x + mod_ref[5:6, :] * _rms(acc, gpost_ref[...])
    for s in range(SLABS):
        for g in lane_groups:
            os_ref[g, pl.ds(s, slab, stride=SLABS), :] = y[s * slab:(s + 1) * slab,
                                                           g * LANES:(g + 1) * LANES]
    for g in lane_groups:
        out_ref[:, g * LANES:(g + 1) * LANES] = os_ref[g]


def _ffn(x1, mod3, g_pre, g_post, w_up, conv_w, conv_b, w_down, seq, tm, chunks):
    assert sum(chunks) == D_FF and all(c % LANES == 0 for c in chunks)
    t, d = x1.shape
    tps = seq // tm
    hpt = tm // HALO
    last = t // HALO - 1
    return pl.pallas_call(
        functools.partial(_ffn_kernel, tps=tps, chunks=chunks),
        grid=(t // tm,),
        in_specs=[pl.BlockSpec((tm, d), lambda i: (i, 0)),
                  pl.BlockSpec((HALO, d), lambda i: (jnp.maximum(i * hpt - 1, 0), 0)),
                  pl.BlockSpec((HALO, d), lambda i: (jnp.minimum((i + 1) * hpt, last), 0)),
                  pl.BlockSpec((None, 6, d), lambda i: (i // tps, 0, 0)),
                  _const_spec((1, d)), _const_spec((1, d)),
                  _const_spec(w_up.shape), _const_spec(conv_w.shape), _const_spec(conv_b.shape),
                  _const_spec(w_down.shape)],
        out_specs=pl.BlockSpec((tm, d), lambda i: (i, 0)),
        out_shape=jax.ShapeDtypeStruct((t, d), F32),
        scratch_shapes=[pltpu.VMEM((2, tm + 2 * HALO, max(chunks)), F32)] * 2
                       + [pltpu.VMEM((d // LANES, tm, LANES), F32)] * 2,
        compiler_params=_params("parallel"),
        name="ffn",
    )(x1, x1, x1, mod3, g_pre, g_post, w_up, conv_w, conv_b, w_down)


def _layer(x, mod3, p, tables, nbr_bias):
    batch, seq, d = x.shape
    x2 = x.reshape(batch * seq, d)
    qkv_a, qkv_b, ga, gb = _inproj(x2, mod3, p["g_mix_pre"], p["w_in"], tables, seq, INPROJ_TM)
    oa = _dilated_attention(qkv_a, batch, seq).reshape(batch * seq, WIDTH)
    ob = _nbr_attention(*qkv_b, nbr_bias, batch, seq)
    x1 = _merge(oa, ob, ga, gb, x2, mod3,
                p["g_mix_post"], p["w_branch_a"], p["w_branch_b"], p["w_out"], seq, MERGE_TM)
    y = _ffn(x1, mod3, p["g_ffn_pre"], p["g_ffn_post"], p["w_up"], p["conv_w"], p["conv_b"],
             p["w_down"], seq, FFN_TM, FFN_CHUNKS)
    return y.reshape(batch, seq, d)


def kernel(x_prompt, x_sample, c_prompt, c_sample, w_ada, b_ada, g_mix_pre, g_mix_post, g_ffn_pre,
           g_ffn_post, w_in, rpb, w_branch_a, w_branch_b, w_out, w_up, conv_w, conv_b, w_down):
    assert w_ada.shape[0] == 1, "single-layer trunk"
    row = lambda a: a[0].reshape(1, -1)
    p = {
        "w_ada": w_ada[0], "b_ada": b_ada[0],
        "g_mix_pre": row(g_mix_pre), "g_mix_post": row(g_mix_post),
        "g_ffn_pre": row(g_ffn_pre), "g_ffn_post": row(g_ffn_post),
        "w_in": w_in[0].astype(BF16),
        "w_branch_a": w_branch_a[0].astype(BF16), "w_branch_b": w_branch_b[0].astype(BF16),
        "w_out": w_out[0].astype(BF16), "w_up": w_up[0].astype(BF16),
        "conv_w": conv_w[0], "conv_b": row(conv_b), "w_down": w_down[0].astype(BF16),
    }
    nbr_bias = _nbr_bias(rpb[0])
    tables = _rope_tables(x_prompt.shape[1])
    n_prompt = x_prompt.shape[0]
    mod = _modulation(jnp.concatenate([c_prompt, c_sample], axis=0), p["w_ada"], p["b_ada"])
    mod3 = mod.reshape(mod.shape[0], 6, D_MODEL)
    y_prompt = _layer(x_prompt, mod3[:n_prompt], p, tables, nbr_bias)
    y_sample = _layer(x_sample, mod3[n_prompt:], p, tables, nbr_bias)
    return (y_prompt, y_sample)
```

```python
import functools

import numpy as np
import jax
import jax.numpy as jnp
from jax import lax
from jax.experimental import pallas as pl
from jax.experimental.pallas import tpu as pltpu

D_MODEL = 1024
HEAD_DIM = 64
N_HEADS = 8
WIDTH = N_HEADS * HEAD_DIM
DILATIONS = (1, 4, 16)
BAND_HALF = 64
ROPE_THETA = 500000.0
ROPE_DIM = HEAD_DIM // 4
GRID_W = 64
NA_ROWS = 8
NA_COLS = 16
D_FF = 2816
EPS = 1e-6
NEG_INF = -1e30
LOG2E = 1.4426950408889634
GELU_C = 0.7978845608028654

LANES = 128
Q_BLOCK = 128
NBR_ROWS_PER_STEP = 8
VMEM_LIMIT = 56 * 1024 * 1024
INPROJ_TM = 512
MERGE_TM = 1024

F32 = jnp.float32
BF16 = jnp.bfloat16


def _params(*sem):
    return pltpu.CompilerParams(dimension_semantics=sem, vmem_limit_bytes=VMEM_LIMIT)


def _const_spec(shape):
    nd = len(shape)
    return pl.BlockSpec(shape, lambda *_: (0,) * nd, pipeline_mode=pl.Buffered(1))


def _rms(x, g):
    return x * lax.rsqrt(jnp.mean(x * x, axis=-1, keepdims=True) + EPS) * g


def _mod_kernel(c_ref, w_ref, b_ref, o_ref):
    c = c_ref[...]
    s = c * jax.nn.sigmoid(c)
    w = w_ref[...]
    s_hi = s.astype(BF16)
    s_lo = (s - s_hi.astype(F32)).astype(BF16)
    w_hi = w.astype(BF16)
    w_lo = (w - w_hi.astype(F32)).astype(BF16)
    acc = jnp.dot(s_hi, w_hi, preferred_element_type=F32)
    acc += jnp.dot(s_hi, w_lo, preferred_element_type=F32)
    acc += jnp.dot(s_lo, w_hi, preferred_element_type=F32)
    o_ref[...] = acc + b_ref[...]


def _modulation(c, w_ada, b_ada):
    b, d = c.shape
    n = w_ada.shape[1]
    return pl.pallas_call(
        _mod_kernel,
        grid=(n // d,),
        in_specs=[_const_spec((b, d)),
                  pl.BlockSpec((d, d), lambda j: (0, j)),
                  pl.BlockSpec((1, d), lambda j: (0, j))],
        out_specs=pl.BlockSpec((b, d), lambda j: (0, j)),
        out_shape=jax.ShapeDtypeStruct((b, n), F32),
        compiler_params=_params("arbitrary"),
        name="mod",
    )(c, w_ada, b_ada.reshape(1, n))


def _inproj_kernel(x_ref, mod_ref, g_ref, w_ref, cos_ref, sa_ref, sb_ref,
                   q1_ref, k1_ref, v1_ref, q4_ref, k4_ref, v4_ref, q16_ref, k16_ref, v16_ref,
                   qb_ref, kb_ref, vb_ref, ga_ref, gb_ref, stage_ref, slab_ref):
    tm = x_ref.shape[0]
    h = _rms(x_ref[...], g_ref[...]) * (1.0 + mod_ref[1:2, :]) + mod_ref[0:1, :]
    hb = h.astype(BF16)

    def proj(c0, width):
        return jnp.dot(hb, w_ref[:, c0:c0 + width], preferred_element_type=F32)

    cos, sa, sb = cos_ref[...], sa_ref[...], sb_ref[...]

    def rope(t):
        outs = []
        for g in range(t.shape[1] // LANES):
            tg = t[:, g * LANES:(g + 1) * LANES]
            outs.append(tg * cos + pltpu.roll(tg, ROPE_DIM // 2, 1) * sa
                        + pltpu.roll(tg, LANES - ROPE_DIM // 2, 1) * sb)
        return jnp.concatenate(outs, axis=1)

    def emit(t, refs):
        ref1, ref4, ref16 = refs
        step = DILATIONS[1]
        assert DILATIONS[2] == step * step
        for g in range(WIDTH // LANES):
            tg = t[:, g * LANES:(g + 1) * LANES]
            ref1[g] = tg.astype(BF16)
            stage_ref[g] = tg
            for r in range(step):
                slab = stage_ref[g, pl.ds(r, tm // step, stride=step), :]
                ref4[g, :, r * LANES:(r + 1) * LANES] = slab.astype(BF16)
                slab_ref[g * step + r] = slab
        for r16 in range(DILATIONS[2]):
            r, u = r16 % step, r16 // step
            for g in range(WIDTH // LANES):
                ref16[g, :, r16 * LANES:(r16 + 1) * LANES] = (
                    slab_ref[g * step + r, pl.ds(u, tm // DILATIONS[2], stride=step), :].astype(BF16))

    scale = HEAD_DIM ** -0.5 * LOG2E

    emit(rope(proj(0, WIDTH)) * scale, (q1_ref, q4_ref, q16_ref))
    emit(rope(proj(WIDTH, WIDTH)), (k1_ref, k4_ref, k16_ref))
    emit(proj(2 * WIDTH, WIDTH), (v1_ref, v4_ref, v16_ref))
    qb_ref[...] = (proj(3 * WIDTH, WIDTH) * scale).astype(BF16)
    kb_ref[...] = proj(4 * WIDTH, WIDTH).astype(BF16)
    vb_ref[...] = proj(5 * WIDTH, WIDTH).astype(BF16)
    ga_ref[...] = jax.nn.sigmoid(proj(6 * WIDTH, D_MODEL)).astype(BF16)
    gb_ref[...] = jax.nn.sigmoid(proj(6 * WIDTH + D_MODEL, D_MODEL)).astype(BF16)


def _rope_tables(seq):
    half = ROPE_DIM // 2
    inv = jnp.power(jnp.float32(ROPE_THETA), -jnp.arange(half, dtype=F32) / half)
    ang = jnp.arange(seq).astype(F32)[:, None] * inv[None, :]
    cos, sin = jnp.cos(ang), jnp.sin(ang)
    ones = jnp.ones((seq, HEAD_DIM - ROPE_DIM), F32)
    zeros = jnp.zeros((seq, HEAD_DIM - ROPE_DIM), F32)
    zh = jnp.zeros((seq, half), F32)
    per_head = lambda *parts: jnp.tile(jnp.concatenate(parts, axis=1), (1, LANES // HEAD_DIM))
    return (per_head(cos, cos, ones),
            per_head(zh, sin, zeros),
            per_head(-sin, zh, zeros))


def _inproj(x2, mod3, g_pre, w_in, tables, seq, tm):
    t, d = x2.shape
    tps = seq // tm
    row = lambda w: pl.BlockSpec((tm, w), lambda i: (i, 0))
    tab = pl.BlockSpec((tm, LANES), lambda i: (i % tps, 0))
    pairs = WIDTH // LANES
    dil_shapes = [jax.ShapeDtypeStruct((t // seq, pairs, seq // dil, dil * LANES), BF16)
                  for dil in DILATIONS]
    dil_specs = [pl.BlockSpec((None, pairs, tm // dil, dil * LANES),
                              lambda i: (i // tps, 0, i % tps, 0)) for dil in DILATIONS]
    outs = pl.pallas_call(
        _inproj_kernel,
        grid=(t // tm,),
        in_specs=[row(d),
                  pl.BlockSpec((None, 6, d), lambda i: (i // tps, 0, 0)),
                  _const_spec((1, d)),
                  _const_spec(w_in.shape),
                  tab, tab, tab],
        out_specs=[s for s in dil_specs for _ in range(3)] + [row(WIDTH)] * 3 + [row(d)] * 2,
        out_shape=[s for s in dil_shapes for _ in range(3)]
                  + [jax.ShapeDtypeStruct((t, WIDTH), BF16)] * 3
                  + [jax.ShapeDtypeStruct((t, d), BF16)] * 2,
        scratch_shapes=[pltpu.VMEM((pairs, tm, LANES), F32),
                        pltpu.VMEM((pairs * DILATIONS[1], tm // DILATIONS[1], LANES), F32)],
        compiler_params=_params("parallel"),
        name="inproj",
    )(x2, mod3, g_pre, w_in, *tables)
    qkv_a = [outs[3 * n:3 * n + 3] for n in range(len(DILATIONS))]
    return qkv_a, outs[9:12], outs[12], outs[13]


def _pair_attention(units, lo_mask, normalise):
    zero = jnp.zeros_like(units[0][0])
    scores = []
    for q2, kwin, _, bias in units:
        qx = jnp.concatenate([jnp.where(lo_mask, q2, zero), jnp.where(lo_mask, zero, q2)], axis=0)
        s = lax.dot_general(qx, kwin, (((1,), (1,)), ((), ())), preferred_element_type=F32)
        scores.append(s + bias)
    maxes = [jnp.max(s, axis=1, keepdims=True) for s in scores]
    probs = [jnp.exp2(s - m).astype(BF16) for s, m in zip(scores, maxes)]
    results = []
    for p, (_, _, vwin, _) in zip(probs, units):
        vaug = jnp.concatenate([vwin, jnp.ones((vwin.shape[0], LANES), BF16)], axis=1)
        results.append(jnp.dot(p, vaug, preferred_element_type=F32))
    outs = []
    for r, m, (q2, _, _, _) in zip(results, maxes, units):
        rows = q2.shape[0]
        o, den = r[:, :LANES], r[:, LANES:]
        if normalise:
            o = o / den
        pick = lambda t: jnp.where(lo_mask, t[:rows], t[rows:])
        outs.append((pick(o), pick(jnp.broadcast_to(m, den.shape)), pick(den)))
    return outs


def _lo_mask(rows):
    return lax.broadcasted_iota(jnp.int32, (rows, LANES), 1) < HEAD_DIM


def _band_units(bias_ref, q_ref, k_ref, v_ref, dil):
    length = q_ref.shape[0]
    nblk = length // Q_BLOCK
    kw = bias_ref.shape[2]
    units, dests = [], []
    for blk in range(nblk):
        i0 = blk * Q_BLOCK
        ks = min(max(i0 - BAND_HALF, 0), length - kw)
        kind = 0 if blk == 0 else (2 if blk == nblk - 1 else 1)
        bias = bias_ref[min(kind, bias_ref.shape[0] - 1)]
        for r in range(dil):
            c = slice(r * LANES, (r + 1) * LANES)
            units.append((q_ref[i0:i0 + Q_BLOCK, c], k_ref[ks:ks + kw, c], v_ref[ks:ks + kw, c], bias))
            dests.append((blk, r))
    return units, dests


def _dilated_kernel(b1_ref, b4_ref, b16_ref, q1_ref, k1_ref, v1_ref, q4_ref, k4_ref, v4_ref,
                    q16_ref, k16_ref, v16_ref, out_ref, o_s, m_s, d_s, slab_s):
    lo = _lo_mask(Q_BLOCK)
    step = DILATIONS[1]
    assert DILATIONS[2] == step * step
    seq = out_ref.shape[0]

    units, dests = _band_units(b4_ref, q4_ref, k4_ref, v4_ref, step)
    for (blk, r), res in zip(dests, _pair_attention(units, lo, False)):
        rows = pl.ds(blk * (Q_BLOCK * step) + r, Q_BLOCK, stride=step)
        for ref, val in zip((o_s, m_s, d_s), res):
            ref[0, rows, :] = val

    units, dests = _band_units(b16_ref, q16_ref, k16_ref, v16_ref, DILATIONS[2])
    for (blk, r16), res in zip(dests, _pair_attention(units, lo, False)):
        r, u = r16 % step, r16 // step
        rows = pl.ds(blk * (Q_BLOCK * step) + u, Q_BLOCK, stride=step)
        for a, val in enumerate(res):
            slab_s[a * step + r, rows, :] = val
    for a, ref in enumerate((o_s, m_s, d_s)):
        for r in range(step):
            ref[1, pl.ds(r, seq // step, stride=step), :] = slab_s[a * step + r]

    units, dests = _band_units(b1_ref, q1_ref, k1_ref, v1_ref, DILATIONS[0])
    for (blk, _), (o1, m1, d1) in zip(dests, _pair_attention(units, lo, False)):
        rows = slice(blk * Q_BLOCK, (blk + 1) * Q_BLOCK)
        o2, m2, d2 = o_s[0, rows, :], m_s[0, rows, :], d_s[0, rows, :]
        o3, m3, d3 = o_s[1, rows, :], m_s[1, rows, :], d_s[1, rows, :]
        m = jnp.maximum(jnp.maximum(m1, m2), m3)
        e1, e2, e3 = jnp.exp2(m1 - m), jnp.exp2(m2 - m), jnp.exp2(m3 - m)
        out_ref[rows, :] = ((e1 * o1 + e2 * o2 + e3 * o3)
                            / (e1 * d1 + e2 * d2 + e3 * d3)).astype(BF16)


def _band_bias(length):
    kw = min(2 * Q_BLOCK, length)
    qi = np.arange(Q_BLOCK)[:, None]
    kj = np.arange(kw)[None, :]
    offsets = (0,) if length == Q_BLOCK else (0, -BAND_HALF, Q_BLOCK - kw)
    tiles = [np.where(np.abs(kj + off - qi) <= BAND_HALF, 0.0, NEG_INF) for off in offsets]
    return jnp.asarray(np.tile(np.stack(tiles), (1, 2, 1)), F32)


def _dilated_attention(qkv_a, batch, seq):
    pairs = WIDTH // LANES
    biases = [_band_bias(seq // dil) for dil in DILATIONS]
    spec = lambda dil: pl.BlockSpec((None, None, seq // dil, dil * LANES), lambda b, g: (b, g, 0, 0))
    return pl.pallas_call(
        _dilated_kernel,
        grid=(batch, pairs),
        in_specs=[_const_spec(b.shape) for b in biases]
                 + [spec(dil) for dil in DILATIONS for _ in range(3)],
        out_specs=pl.BlockSpec((None, seq, LANES), lambda b, g: (b, 0, g)),
        out_shape=jax.ShapeDtypeStruct((batch, seq, WIDTH), BF16),
        scratch_shapes=[pltpu.VMEM((len(DILATIONS) - 1, seq, LANES), F32)] * 3
                       + [pltpu.VMEM((3 * DILATIONS[1], seq // DILATIONS[1], LANES), F32)],
        compiler_params=_params("parallel", "parallel"),
        name="dilated",
    )(*biases, *[a for qkv in qkv_a for a in qkv])


def _nbr_kernel(bias_ref, q_ref, k_ref, v_ref, o_ref, *, rows):
    lo = _lo_mask(GRID_W)
    nkeys = NA_ROWS * GRID_W

    pairs = WIDTH // LANES
    rps = NBR_ROWS_PER_STEP

    def step(it, carry):
        units, dests = [], []
        for j in range(rps):
            r = it * rps + j
            rs = jnp.clip(r - NA_ROWS // 2, 0, rows - NA_ROWS)
            var = r - rs
            q0 = pl.multiple_of(r * GRID_W, GRID_W)
            k0 = pl.multiple_of(rs * GRID_W, GRID_W)
            for p in range(pairs):
                c = slice(p * LANES, (p + 1) * LANES)
                units.append((q_ref[pl.ds(q0, GRID_W), c], k_ref[pl.ds(k0, nkeys), c],
                              v_ref[pl.ds(k0, nkeys), c], bias_ref[p * NA_ROWS + var]))
                dests.append((q0, c))
        for (q0, c), (o, _, _) in zip(dests, _pair_attention(units, lo, True)):
            o_ref[pl.ds(q0, GRID_W), c] = o.astype(BF16)
        return carry

    lax.fori_loop(0, rows // rps, step, 0)


def _nbr_bias(rpb):
    c = np.arange(GRID_W)[:, None]
    ck = np.arange(GRID_W)[None, :]
    cs = np.clip(c - NA_COLS // 2, 0, GRID_W - NA_COLS)
    ok = (ck >= cs) & (ck < cs + NA_COLS)
    pad = GRID_W - NA_COLS
    rp = jnp.pad(rpb.astype(F32) * LOG2E, ((0, 0), (0, 0), (pad, pad)))
    t = jnp.stack([rp[:, :, GRID_W - 1 - q:2 * GRID_W - 1 - q] for q in range(GRID_W)], axis=1)
    t = jnp.where(ok[None, :, None, :], t, NEG_INF)
    t = t.reshape(N_HEADS // 2, 2 * GRID_W, 2 * NA_ROWS - 1, GRID_W)
    b = jnp.stack([t[:, :, NA_ROWS - 1 - v:2 * NA_ROWS - 1 - v] for v in range(NA_ROWS)], axis=1)
    return b.reshape(N_HEADS // 2 * NA_ROWS, 2 * GRID_W, NA_ROWS * GRID_W)


def _nbr_attention(q, k, v, bias, batch, seq):
    view = lambda a: a.reshape(batch, seq, WIDTH)
    spec = pl.BlockSpec((None, seq, WIDTH), lambda b: (b, 0, 0))
    o = pl.pallas_call(
        functools.partial(_nbr_kernel, rows=seq // GRID_W),
        grid=(batch,),
        in_specs=[_const_spec(bias.shape), spec, spec, spec],
        out_specs=spec,
        out_shape=jax.ShapeDtypeStruct((batch, seq, WIDTH), BF16),
        compiler_params=_params("parallel"),
        name="nbr",
    )(bias, view(q), view(k), view(v))
    return o.reshape(batch * seq, WIDTH)


def _merge_kernel(oa_ref, ob_ref, ga_ref, gb_ref,
                  x_ref, mod_ref, g_ref, wa_ref, wb_ref, wo_ref, out_ref):
    ya = jnp.dot(oa_ref[...], wa_ref[...], preferred_element_type=F32)
    yb = jnp.dot(ob_ref[...], wb_ref[...], preferred_element_type=F32)
    merged = ga_ref[...].astype(F32) * ya + gb_ref[...].astype(F32) * yb
    z = jnp.dot(merged.astype(BF16), wo_ref[...], preferred_element_type=F32)
    out_ref[...] = x_ref[...] + mod_ref[2:3, :] * _rms(z, g_ref[...])


def _merge(oa, ob, ga, gb, x2, mod3, g_post, wa, wb, wo, seq, tm):
    t, d = x2.shape
    tps = seq // tm
    row = lambda w: pl.BlockSpec((tm, w), lambda i: (i, 0))
    return pl.pallas_call(
        _merge_kernel,
        grid=(t // tm,),
        in_specs=[row(WIDTH)] * 2 + [row(d)] * 3
                 + [pl.BlockSpec((None, 6, d), lambda i: (i // tps, 0, 0)),
                    _const_spec((1, d)), _const_spec(wa.shape), _const_spec(wb.shape),
                    _const_spec(wo.shape)],
        out_specs=row(d),
        out_shape=jax.ShapeDtypeStruct((t, d), F32),
        compiler_params=_params("parallel"),
        name="merge",
    )(oa, ob, ga, gb, x2, mod3, g_post, wa, wb, wo)


HALO = 16
FFN_TM = 512
SLABS = 8
FFN_CHUNKS = (256, 768, 768, 768, 256)


def _ffn_kernel(x_ref, xp_ref, xn_ref, mod_ref, gpre_ref, gpost_ref, wup_ref, cw_ref, cb_ref,
                wdn_ref, out_ref, uv_ref, ug_ref, xs_ref, os_ref, *, tps, chunks):
    tm = x_ref.shape[0]
    slab = tm // SLABS
    lane_groups = range(D_MODEL // LANES)
    i = pl.program_id(0)
    has_prev = (i % tps != 0).astype(F32)
    has_next = (i % tps != tps - 1).astype(F32)
    shift, gain, gpre = mod_ref[3:4, :], 1.0 + mod_ref[4:5, :], gpre_ref[...]

    def prenorm(x):
        return _rms(x, gpre) * gain + shift

    for g in lane_groups:
        xs_ref[g] = x_ref[:, g * LANES:(g + 1) * LANES]
    x = jnp.concatenate(
        [jnp.concatenate([xs_ref[g, pl.ds(j, slab, stride=SLABS), :] for g in lane_groups], axis=1)
         for j in range(SLABS)], axis=0)
    first = lax.broadcasted_iota(jnp.int32, (HALO, D_MODEL), 0) == 0
    halo = jnp.where(first, xp_ref[HALO - 1:HALO, :], xn_ref[0:1, :])
    hext = jnp.concatenate([prenorm(x).astype(BF16),
                            (prenorm(halo) * jnp.where(first, has_prev, has_next)).astype(BF16)], axis=0)
    prev_row, next_row = tm, tm + 1
    starts = [sum(chunks[:j]) for j in range(len(chunks))]

    def up(j):
        c0, w = starts[j], chunks[j]
        uv_ref[j % 2, :, 0:w] = jnp.dot(hext, wup_ref[:, c0:c0 + w], preferred_element_type=F32)
        ug_ref[j % 2, :, 0:w] = jnp.dot(hext, wup_ref[:, D_FF + c0:D_FF + c0 + w],
                                        preferred_element_type=F32)

    def activation(j):
        c0, w = starts[j], chunks[j]

        row = lax.broadcasted_iota(jnp.int32, (slab, w), 0)

        def conv(u_ref, col0, gain):
            cols = slice(col0, col0 + w)
            w0, w1, w2, b = (gain * t for t in (cw_ref[0:1, cols], cw_ref[1:2, cols],
                                                cw_ref[2:3, cols], cb_ref[:, cols]))
            rows = lambda r0: u_ref[j % 2, pl.ds(r0, slab), 0:w]
            out = []
            for s in range(SLABS):
                prev = rows((s - 1) * slab) if s > 0 else jnp.where(
                    row == 0, u_ref[j % 2, prev_row:prev_row + 1, 0:w], rows((SLABS - 1) * slab - 1))
                nxt = rows((s + 1) * slab) if s < SLABS - 1 else jnp.where(
                    row == slab - 1, u_ref[j % 2, next_row:next_row + 1, 0:w], rows(1))
                out.append(w0 * prev + w1 * rows(s * slab) + w2 * nxt + b)
            return jnp.concatenate(out, axis=0)

        g = conv(ug_ref, D_FF + c0, 1.0)
        gv = g * conv(uv_ref, c0, 0.5)
        t = jnp.tanh(g * (GELU_C + (GELU_C * 0.044715) * (g * g)))
        return (gv + gv * t).astype(BF16)

    up(0)
    acc = jnp.zeros((tm, D_MODEL), F32)
    for j in range(len(chunks)):
        if j + 1 < len(chunks):
            up(j + 1)
        c0, w = starts[j], chunks[j]
        acc += jnp.dot(activation(j), wdn_ref[c0:c0 + w, :], preferred_element_type=F32)
    y = x + mod_ref[5:6, :] * _rms(acc, gpost_ref[...])
    for s in range(SLABS):
        for g in lane_groups:
            os_ref[g, pl.ds(s, slab, stride=SLABS), :] = y[s * slab:(s + 1) * slab,
                                                           g * LANES:(g + 1) * LANES]
    for g in lane_groups:
        out_ref[:, g * LANES:(g + 1) * LANES] = os_ref[g]


def _ffn(x1, mod3, g_pre, g_post, w_up, conv_w, conv_b, w_down, seq, tm, chunks):
    assert sum(chunks) == D_FF and all(c % LANES == 0 for c in chunks)
    t, d = x1.shape
    tps = seq // tm
    hpt = tm // HALO
    last = t // HALO - 1
    return pl.pallas_call(
        functools.partial(_ffn_kernel, tps=tps, chunks=chunks),
        grid=(t // tm,),
        in_specs=[pl.BlockSpec((tm, d), lambda i: (i, 0)),
                  pl.BlockSpec((HALO, d), lambda i: (jnp.maximum(i * hpt - 1, 0), 0)),
                  pl.BlockSpec((HALO, d), lambda i: (jnp.minimum((i + 1) * hpt, last), 0)),
                  pl.BlockSpec((None, 6, d), lambda i: (i // tps, 0, 0)),
                  _const_spec((1, d)), _const_spec((1, d)),
                  _const_spec(w_up.shape), _const_spec(conv_w.shape), _const_spec(conv_b.shape),
                  _const_spec(w_down.shape)],
        out_specs=pl.BlockSpec((tm, d), lambda i: (i, 0)),
        out_shape=jax.ShapeDtypeStruct((t, d), F32),
        scratch_shapes=[pltpu.VMEM((2, tm + HALO, max(chunks)), F32)] * 2
                       + [pltpu.VMEM((d // LANES, tm, LANES), F32)] * 2,
        compiler_params=_params("parallel"),
        name="ffn",
    )(x1, x1, x1, mod3, g_pre, g_post, w_up, conv_w, conv_b, w_down)


def _layer(x, mod3, p, tables, nbr_bias):
    batch, seq, d = x.shape
    x2 = x.reshape(batch * seq, d)
    qkv_a, qkv_b, ga, gb = _inproj(x2, mod3, p["g_mix_pre"], p["w_in"], tables, seq, INPROJ_TM)
    oa = _dilated_attention(qkv_a, batch, seq).reshape(batch * seq, WIDTH)
    ob = _nbr_attention(*qkv_b, nbr_bias, batch, seq)
    x1 = _merge(oa, ob, ga, gb, x2, mod3,
                p["g_mix_post"], p["w_branch_a"], p["w_branch_b"], p["w_out"], seq, MERGE_TM)
    y = _ffn(x1, mod3, p["g_ffn_pre"], p["g_ffn_post"], p["w_up"], p["conv_w"], p["conv_b"],
             p["w_down"], seq, FFN_TM, FFN_CHUNKS)
    return y.reshape(batch, seq, d)


def kernel(x_prompt, x_sample, c_prompt, c_sample, w_ada, b_ada, g_mix_pre, g_mix_post, g_ffn_pre,
           g_ffn_post, w_in, rpb, w_branch_a, w_branch_b, w_out, w_up, conv_w, conv_b, w_down):
    assert w_ada.shape[0] == 1, "single-layer trunk"
    row = lambda a: a[0].reshape(1, -1)
    p = {
        "w_ada": w_ada[0], "b_ada": b_ada[0],
        "g_mix_pre": row(g_mix_pre), "g_mix_post": row(g_mix_post),
        "g_ffn_pre": row(g_ffn_pre), "g_ffn_post": row(g_ffn_post),
        "w_in": w_in[0].astype(BF16),
        "w_branch_a": w_branch_a[0].astype(BF16), "w_branch_b": w_branch_b[0].astype(BF16),
        "w_out": w_out[0].astype(BF16), "w_up": w_up[0].astype(BF16),
        "conv_w": conv_w[0], "conv_b": row(conv_b), "w_down": w_down[0].astype(BF16),
    }
    nbr_bias = _nbr_bias(rpb[0])
    tables = _rope_tables(x_prompt.shape[1])
    n_prompt = x_prompt.shape[0]
    mod = _modulation(jnp.concatenate([c_prompt, c_sample], axis=0), p["w_ada"], p["b_ada"])
    mod3 = mod.reshape(mod.shape[0], 6, D_MODEL)
    y_prompt = _layer(x_prompt, mod3[:n_prompt], p, tables, nbr_bias)
    y_sample = _layer(x_sample, mod3[n_prompt:], p, tables, nbr_bias)
    return (y_prompt, y_sample)
```

```python
import functools

import numpy as np
import jax
import jax.numpy as jnp
from jax import lax
from jax.experimental import pallas as pl
from jax.experimental.pallas import tpu as pltpu

D_MODEL = 1024
HEAD_DIM = 64
N_HEADS = 8
WIDTH = N_HEADS * HEAD_DIM
DILATIONS = (1, 4, 16)
BAND_HALF = 64
ROPE_THETA = 500000.0
ROPE_DIM = HEAD_DIM // 4
GRID_W = 64
NA_ROWS = 8
NA_COLS = 16
D_FF = 2816
EPS = 1e-6
NEG_INF = -1e30
LOG2E = 1.4426950408889634
GELU_C = 0.7978845608028654

LANES = 128
Q_BLOCK = 128
NBR_ROWS_PER_STEP = 8
VMEM_LIMIT = 56 * 1024 * 1024
INPROJ_TM = 512
MERGE_TM = 1024

F32 = jnp.float32
BF16 = jnp.bfloat16


def _params(*sem):
    return pltpu.CompilerParams(dimension_semantics=sem, vmem_limit_bytes=VMEM_LIMIT)


def _const_spec(shape):
    nd = len(shape)
    return pl.BlockSpec(shape, lambda *_: (0,) * nd, pipeline_mode=pl.Buffered(1))


def _rms(x, g):
    return x * lax.rsqrt(jnp.mean(x * x, axis=-1, keepdims=True) + EPS) * g


def _mod_kernel(c_ref, w_ref, b_ref, o_ref):
    c = c_ref[...]
    s = c * jax.nn.sigmoid(c)
    w = w_ref[...]
    s_hi = s.astype(BF16)
    s_lo = (s - s_hi.astype(F32)).astype(BF16)
    w_hi = w.astype(BF16)
    w_lo = (w - w_hi.astype(F32)).astype(BF16)
    acc = jnp.dot(s_hi, w_hi, preferred_element_type=F32)
    acc += jnp.dot(s_hi, w_lo, preferred_element_type=F32)
    acc += jnp.dot(s_lo, w_hi, preferred_element_type=F32)
    o_ref[...] = acc + b_ref[...]


def _modulation(c, w_ada, b_ada):
    b, d = c.shape
    n = w_ada.shape[1]
    return pl.pallas_call(
        _mod_kernel,
        grid=(n // d,),
        in_specs=[_const_spec((b, d)),
                  pl.BlockSpec((d, d), lambda j: (0, j)),
                  pl.BlockSpec((1, d), lambda j: (0, j))],
        out_specs=pl.BlockSpec((b, d), lambda j: (0, j)),
        out_shape=jax.ShapeDtypeStruct((b, n), F32),
        compiler_params=_params("arbitrary"),
        name="mod",
    )(c, w_ada, b_ada.reshape(1, n))


def _inproj_kernel(x_ref, mod_ref, g_ref, w_ref, cos_ref, sa_ref, sb_ref,
                   q1_ref, k1_ref, v1_ref, q4_ref, k4_ref, v4_ref, q16_ref, k16_ref, v16_ref,
                   qb_ref, kb_ref, vb_ref, ga_ref, gb_ref, stage_ref, slab_ref):
    tm = x_ref.shape[0]
    h = _rms(x_ref[...], g_ref[...]) * (1.0 + mod_ref[1:2, :]) + mod_ref[0:1, :]
    hb = h.astype(BF16)

    def proj(c0, width):
        return jnp.dot(hb, w_ref[:, c0:c0 + width], preferred_element_type=F32)

    cos, sa, sb = cos_ref[...], sa_ref[...], sb_ref[...]

    def rope(t):
        outs = []
        for g in range(t.shape[1] // LANES):
            tg = t[:, g * LANES:(g + 1) * LANES]
            outs.append(tg * cos + pltpu.roll(tg, ROPE_DIM // 2, 1) * sa
                        + pltpu.roll(tg, LANES - ROPE_DIM // 2, 1) * sb)
        return jnp.concatenate(outs, axis=1)

    def emit(t, refs):
        ref1, ref4, ref16 = refs
        step = DILATIONS[1]
        assert DILATIONS[2] == step * step
        for g in range(WIDTH // LANES):
            tg = t[:, g * LANES:(g + 1) * LANES]
            ref1[g] = tg.astype(BF16)
            stage_ref[g] = tg
            for r in range(step):
                slab = stage_ref[g, pl.ds(r, tm // step, stride=step), :]
                ref4[g, :, r * LANES:(r + 1) * LANES] = slab.astype(BF16)
                slab_ref[g * step + r] = slab
        for r16 in range(DILATIONS[2]):
            r, u = r16 % step, r16 // step
            for g in range(WIDTH // LANES):
                ref16[g, :, r16 * LANES:(r16 + 1) * LANES] = (
                    slab_ref[g * step + r, pl.ds(u, tm // DILATIONS[2], stride=step), :].astype(BF16))

    scale = HEAD_DIM ** -0.5 * LOG2E

    emit(rope(proj(0, WIDTH)) * scale, (q1_ref, q4_ref, q16_ref))
    emit(rope(proj(WIDTH, WIDTH)), (k1_ref, k4_ref, k16_ref))
    emit(proj(2 * WIDTH, WIDTH), (v1_ref, v4_ref, v16_ref))
    qb_ref[...] = (proj(3 * WIDTH, WIDTH) * scale).astype(BF16)
    kb_ref[...] = proj(4 * WIDTH, WIDTH).astype(BF16)
    vb_ref[...] = proj(5 * WIDTH, WIDTH).astype(BF16)
    ga_ref[...] = jax.nn.sigmoid(proj(6 * WIDTH, D_MODEL)).astype(BF16)
    gb_ref[...] = jax.nn.sigmoid(proj(6 * WIDTH + D_MODEL, D_MODEL)).astype(BF16)


def _rope_tables(seq):
    half = ROPE_DIM // 2
    inv = jnp.power(jnp.float32(ROPE_THETA), -jnp.arange(half, dtype=F32) / half)
    ang = jnp.arange(seq).astype(F32)[:, None] * inv[None, :]
    cos, sin = jnp.cos(ang), jnp.sin(ang)
    ones = jnp.ones((seq, HEAD_DIM - ROPE_DIM), F32)
    zeros = jnp.zeros((seq, HEAD_DIM - ROPE_DIM), F32)
    zh = jnp.zeros((seq, half), F32)
    per_head = lambda *parts: jnp.tile(jnp.concatenate(parts, axis=1), (1, LANES // HEAD_DIM))
    return (per_head(cos, cos, ones),
            per_head(zh, sin, zeros),
            per_head(-sin, zh, zeros))


def _inproj(x2, mod3, g_pre, w_in, tables, seq, tm):
    t, d = x2.shape
    tps = seq // tm
    row = lambda w: pl.BlockSpec((tm, w), lambda i: (i, 0))
    tab = pl.BlockSpec((tm, LANES), lambda i: (i % tps, 0))
    pairs = WIDTH // LANES
    dil_shapes = [jax.ShapeDtypeStruct((t // seq, pairs, seq // dil, dil * LANES), BF16)
                  for dil in DILATIONS]
    dil_specs = [pl.BlockSpec((None, pairs, tm // dil, dil * LANES),
                              lambda i: (i // tps, 0, i % tps, 0)) for dil in DILATIONS]
    outs = pl.pallas_call(
        _inproj_kernel,
        grid=(t // tm,),
        in_specs=[row(d),
                  pl.BlockSpec((None, 6, d), lambda i: (i // tps, 0, 0)),
                  _const_spec((1, d)),
                  _const_spec(w_in.shape),
                  tab, tab, tab],
        out_specs=[s for s in dil_specs for _ in range(3)] + [row(WIDTH)] * 3 + [row(d)] * 2,
        out_shape=[s for s in dil_shapes for _ in range(3)]
                  + [jax.ShapeDtypeStruct((t, WIDTH), BF16)] * 3
                  + [jax.ShapeDtypeStruct((t, d), BF16)] * 2,
        scratch_shapes=[pltpu.VMEM((pairs, tm, LANES), F32),
                        pltpu.VMEM((pairs * DILATIONS[1], tm // DILATIONS[1], LANES), F32)],
        compiler_params=_params("parallel"),
        name="inproj",
    )(x2, mod3, g_pre, w_in, *tables)
    qkv_a = [outs[3 * n:3 * n + 3] for n in range(len(DILATIONS))]
    return qkv_a, outs[9:12], outs[12], outs[13]


def _pair_attention(units, lo_mask, normalise):
    zero = jnp.zeros_like(units[0][0])
    scores = []
    for q2, kwin, _, bias in units:
        qx = jnp.concatenate([jnp.where(lo_mask, q2, zero), jnp.where(lo_mask, zero, q2)], axis=0)
        s = lax.dot_general(qx, kwin, (((1,), (1,)), ((), ())), preferred_element_type=F32)
        scores.append(s + bias)
    maxes = [jnp.max(s, axis=1, keepdims=True) for s in scores]
    probs = [jnp.exp2(s - m).astype(BF16) for s, m in zip(scores, maxes)]
    for p, m, (q2, _, vwin, _) in zip(probs, maxes, units):
        rows = q2.shape[0]
        vaug = jnp.concatenate([vwin, jnp.ones((vwin.shape[0], LANES), BF16)], axis=1)
        r = jnp.dot(p, vaug, preferred_element_type=F32)
        o, den = r[:, :LANES], r[:, LANES:]
        if normalise:
            o = o / den
        pick = lambda t: jnp.where(lo_mask, t[:rows], t[rows:])
        yield pick(o), pick(jnp.broadcast_to(m, den.shape)), pick(den)


def _lo_mask(rows):
    return lax.broadcasted_iota(jnp.int32, (rows, LANES), 1) < HEAD_DIM


def _band_units(bias_ref, q_ref, k_ref, v_ref, dil):
    length = q_ref.shape[0]
    nblk = length // Q_BLOCK
    kw = bias_ref.shape[2]
    units, dests = [], []
    for blk in range(nblk):
        i0 = blk * Q_BLOCK
        ks = min(max(i0 - BAND_HALF, 0), length - kw)
        kind = 0 if blk == 0 else (2 if blk == nblk - 1 else 1)
        bias = bias_ref[min(kind, bias_ref.shape[0] - 1)]
        for r in range(dil):
            c = slice(r * LANES, (r + 1) * LANES)
            units.append((q_ref[i0:i0 + Q_BLOCK, c], k_ref[ks:ks + kw, c], v_ref[ks:ks + kw, c], bias))
            dests.append((blk, r))
    return units, dests


def _dilated_kernel(b1_ref, b4_ref, b16_ref, q1_ref, k1_ref, v1_ref, q4_ref, k4_ref, v4_ref,
                    q16_ref, k16_ref, v16_ref, out_ref, o_s, m_s, d_s, slab_s):
    lo = _lo_mask(Q_BLOCK)
    step = DILATIONS[1]
    assert DILATIONS[2] == step * step
    seq = out_ref.shape[0]

    units, dests = _band_units(b4_ref, q4_ref, k4_ref, v4_ref, step)
    for (blk, r), res in zip(dests, _pair_attention(units, lo, False)):
        rows = pl.ds(blk * (Q_BLOCK * step) + r, Q_BLOCK, stride=step)
        for ref, val in zip((o_s, m_s, d_s), res):
            ref[0, rows, :] = val

    units, dests = _band_units(b16_ref, q16_ref, k16_ref, v16_ref, DILATIONS[2])
    for (blk, r16), res in zip(dests, _pair_attention(units, lo, False)):
        r, u = r16 % step, r16 // step
        rows = pl.ds(blk * (Q_BLOCK * step) + u, Q_BLOCK, stride=step)
        for a, val in enumerate(res):
            slab_s[a * step + r, rows, :] = val
    for a, ref in enumerate((o_s, m_s, d_s)):
        for r in range(step):
            ref[1, pl.ds(r, seq // step, stride=step), :] = slab_s[a * step + r]

    units, dests = _band_units(b1_ref, q1_ref, k1_ref, v1_ref, DILATIONS[0])
    for (blk, _), (o1, m1, d1) in zip(dests, _pair_attention(units, lo, False)):
        rows = slice(blk * Q_BLOCK, (blk + 1) * Q_BLOCK)
        o2, m2, d2 = o_s[0, rows, :], m_s[0, rows, :], d_s[0, rows, :]
        o3, m3, d3 = o_s[1, rows, :], m_s[1, rows, :], d_s[1, rows, :]
        m = jnp.maximum(jnp.maximum(m1, m2), m3)
        e1, e2, e3 = jnp.exp2(m1 - m), jnp.exp2(m2 - m), jnp.exp2(m3 - m)
        out_ref[rows, :] = ((e1 * o1 + e2 * o2 + e3 * o3)
                            / (e1 * d1 + e2 * d2 + e3 * d3)).astype(BF16)


def _band_bias(length):
    kw = min(2 * Q_BLOCK, length)
    qi = np.arange(Q_BLOCK)[:, None]
    kj = np.arange(kw)[None, :]
    offsets = (0,) if length == Q_BLOCK else (0, -BAND_HALF, Q_BLOCK - kw)
    tiles = [np.where(np.abs(kj + off - qi) <= BAND_HALF, 0.0, NEG_INF) for off in offsets]
    return jnp.asarray(np.tile(np.stack(tiles), (1, 2, 1)), F32)


def _dilated_attention(qkv_a, batch, seq):
    pairs = WIDTH // LANES
    biases = [_band_bias(seq // dil) for dil in DILATIONS]
    spec = lambda dil: pl.BlockSpec((None, None, seq // dil, dil * LANES), lambda b, g: (b, g, 0, 0))
    return pl.pallas_call(
        _dilated_kernel,
        grid=(batch, pairs),
        in_specs=[_const_spec(b.shape) for b in biases]
                 + [spec(dil) for dil in DILATIONS for _ in range(3)],
        out_specs=pl.BlockSpec((None, seq, LANES), lambda b, g: (b, 0, g)),
        out_shape=jax.ShapeDtypeStruct((batch, seq, WIDTH), BF16),
        scratch_shapes=[pltpu.VMEM((len(DILATIONS) - 1, seq, LANES), F32)] * 3
                       + [pltpu.VMEM((3 * DILATIONS[1], seq // DILATIONS[1], LANES), F32)],
        compiler_params=_params("parallel", "parallel"),
        name="dilated",
    )(*biases, *[a for qkv in qkv_a for a in qkv])


def _nbr_kernel(bias_ref, q_ref, k_ref, v_ref, o_ref, *, rows):
    lo = _lo_mask(GRID_W)
    nkeys = NA_ROWS * GRID_W

    pairs = WIDTH // LANES
    rps = NBR_ROWS_PER_STEP

    def step(it, carry):
        units, dests = [], []
        for j in range(rps):
            r = it * rps + j
            rs = jnp.clip(r - NA_ROWS // 2, 0, rows - NA_ROWS)
            var = r - rs
            q0 = pl.multiple_of(r * GRID_W, GRID_W)
            k0 = pl.multiple_of(rs * GRID_W, GRID_W)
            for p in range(pairs):
                c = slice(p * LANES, (p + 1) * LANES)
                units.append((q_ref[pl.ds(q0, GRID_W), c], k_ref[pl.ds(k0, nkeys), c],
                              v_ref[pl.ds(k0, nkeys), c], bias_ref[p * NA_ROWS + var]))
                dests.append((q0, c))
        for (q0, c), (o, _, _) in zip(dests, _pair_attention(units, lo, True)):
            o_ref[pl.ds(q0, GRID_W), c] = o.astype(BF16)
        return carry

    lax.fori_loop(0, rows // rps, step, 0)


def _nbr_bias(rpb):
    c = np.arange(GRID_W)[:, None]
    ck = np.arange(GRID_W)[None, :]
    cs = np.clip(c - NA_COLS // 2, 0, GRID_W - NA_COLS)
    ok = (ck >= cs) & (ck < cs + NA_COLS)
    pad = GRID_W - NA_COLS
    rp = jnp.pad(rpb.astype(F32) * LOG2E, ((0, 0), (0, 0), (pad, pad)))
    t = jnp.stack([rp[:, :, GRID_W - 1 - q:2 * GRID_W - 1 - q] for q in range(GRID_W)], axis=1)
    t = jnp.where(ok[None, :, None, :], t, NEG_INF)
    t = t.reshape(N_HEADS // 2, 2 * GRID_W, 2 * NA_ROWS - 1, GRID_W)
    b = jnp.stack([t[:, :, NA_ROWS - 1 - v:2 * NA_ROWS - 1 - v] for v in range(NA_ROWS)], axis=1)
    return b.reshape(N_HEADS // 2 * NA_ROWS, 2 * GRID_W, NA_ROWS * GRID_W)


def _nbr_attention(q, k, v, bias, batch, seq):
    view = lambda a: a.reshape(batch, seq, WIDTH)
    spec = pl.BlockSpec((None, seq, WIDTH), lambda b: (b, 0, 0))
    o = pl.pallas_call(
        functools.partial(_nbr_kernel, rows=seq // GRID_W),
        grid=(batch,),
        in_specs=[_const_spec(bias.shape), spec, spec, spec],
        out_specs=spec,
        out_shape=jax.ShapeDtypeStruct((batch, seq, WIDTH), BF16),
        compiler_params=_params("parallel"),
        name="nbr",
    )(bias, view(q), view(k), view(v))
    return o.reshape(batch * seq, WIDTH)


def _merge_kernel(oa_ref, ob_ref, ga_ref, gb_ref,
                  x_ref, mod_ref, g_ref, wa_ref, wb_ref, wo_ref, out_ref):
    ya = jnp.dot(oa_ref[...], wa_ref[...], preferred_element_type=F32)
    yb = jnp.dot(ob_ref[...], wb_ref[...], preferred_element_type=F32)
    merged = ga_ref[...].astype(F32) * ya + gb_ref[...].astype(F32) * yb
    z = jnp.dot(merged.astype(BF16), wo_ref[...], preferred_element_type=F32)
    out_ref[...] = x_ref[...] + mod_ref[2:3, :] * _rms(z, g_ref[...])


def _merge(oa, ob, ga, gb, x2, mod3, g_post, wa, wb, wo, seq, tm):
    t, d = x2.shape
    tps = seq // tm
    row = lambda w: pl.BlockSpec((tm, w), lambda i: (i, 0))
    return pl.pallas_call(
        _merge_kernel,
        grid=(t // tm,),
        in_specs=[row(WIDTH)] * 2 + [row(d)] * 3
                 + [pl.BlockSpec((None, 6, d), lambda i: (i // tps, 0, 0)),
                    _const_spec((1, d)), _const_spec(wa.shape), _const_spec(wb.shape),
                    _const_spec(wo.shape)],
        out_specs=row(d),
        out_shape=jax.ShapeDtypeStruct((t, d), F32),
        compiler_params=_params("parallel"),
        name="merge",
    )(oa, ob, ga, gb, x2, mod3, g_post, wa, wb, wo)


HALO = 16
FFN_TM = 512
FFN_SLOTS = 3
SLABS = 8
FFN_CHUNKS = (256, 768, 768, 768, 256)


def _ffn_kernel(x_ref, xp_ref, xn_ref, mod_ref, gpre_ref, gpost_ref, wup_ref, cw_ref, cb_ref,
                wdn_ref, out_ref, uv_ref, ug_ref, xs_ref, os_ref, *, tps, chunks):
    tm = x_ref.shape[0]
    slab = tm // SLABS
    lane_groups = range(D_MODEL // LANES)
    i = pl.program_id(0)
    has_prev = (i % tps != 0).astype(F32)
    has_next = (i % tps != tps - 1).astype(F32)
    shift, gain, gpre = mod_ref[3:4, :], 1.0 + mod_ref[4:5, :], gpre_ref[...]

    def prenorm(x):
        return _rms(x, gpre) * gain + shift

    for g in lane_groups:
        xs_ref[g] = x_ref[:, g * LANES:(g + 1) * LANES]
    x = jnp.concatenate(
        [jnp.concatenate([xs_ref[g, pl.ds(j, slab, stride=SLABS), :] for g in lane_groups], axis=1)
         for j in range(SLABS)], axis=0)
    first = lax.broadcasted_iota(jnp.int32, (HALO, D_MODEL), 0) == 0
    halo = jnp.where(first, xp_ref[HALO - 1:HALO, :], xn_ref[0:1, :])
    hext = jnp.concatenate([prenorm(x).astype(BF16),
                            (prenorm(halo) * jnp.where(first, has_prev, has_next)).astype(BF16)], axis=0)
    prev_row, next_row = tm, tm + 1
    starts = [sum(chunks[:j]) for j in range(len(chunks))]

    def up(j):
        c0, w = starts[j], chunks[j]
        uv_ref[j % FFN_SLOTS, :, 0:w] = jnp.dot(hext, wup_ref[:, c0:c0 + w], preferred_element_type=F32)
        ug_ref[j % FFN_SLOTS, :, 0:w] = jnp.dot(hext, wup_ref[:, D_FF + c0:D_FF + c0 + w],
                                        preferred_element_type=F32)

    def activation(j):
        c0, w = starts[j], chunks[j]

        row = lax.broadcasted_iota(jnp.int32, (slab, w), 0)

        def conv(u_ref, col0, gain):
            cols = slice(col0, col0 + w)
            w0, w1, w2, b = (gain * t for t in (cw_ref[0:1, cols], cw_ref[1:2, cols],
                                                cw_ref[2:3, cols], cb_ref[:, cols]))
            rows = lambda r0: u_ref[j % FFN_SLOTS, pl.ds(r0, slab), 0:w]
            out = []
            for s in range(SLABS):
                prev = rows((s - 1) * slab) if s > 0 else jnp.where(
                    row == 0, u_ref[j % FFN_SLOTS, prev_row:prev_row + 1, 0:w], rows((SLABS - 1) * slab - 1))
                nxt = rows((s + 1) * slab) if s < SLABS - 1 else jnp.where(
                    row == slab - 1, u_ref[j % FFN_SLOTS, next_row:next_row + 1, 0:w], rows(1))
                out.append(w0 * prev + w1 * rows(s * slab) + w2 * nxt + b)
            return jnp.concatenate(out, axis=0)

        g = conv(ug_ref, D_FF + c0, 1.0)
        gv = g * conv(uv_ref, c0, 0.5)
        t = jnp.tanh(g * (GELU_C + (GELU_C * 0.044715) * (g * g)))
        return (gv + gv * t).astype(BF16)

    ahead = FFN_SLOTS - 1
    for j in range(min(ahead, len(chunks))):
        up(j)
    acc = jnp.zeros((tm, D_MODEL), F32)
    for j in range(len(chunks)):
        if j + ahead < len(chunks):
            up(j + ahead)
        c0, w = starts[j], chunks[j]
        acc += jnp.dot(activation(j), wdn_ref[c0:c0 + w, :], preferred_element_type=F32)
    y = x + mod_ref[5:6, :] * _rms(acc, gpost_ref[...])
    for s in range(SLABS):
        for g in lane_groups:
            os_ref[g, pl.ds(s, slab, stride=SLABS), :] = y[s * slab:(s + 1) * slab,
                                                           g * LANES:(g + 1) * LANES]
    for g in lane_groups:
        out_ref[:, g * LANES:(g + 1) * LANES] = os_ref[g]


def _ffn(x1, mod3, g_pre, g_post, w_up, conv_w, conv_b, w_down, seq, tm, chunks):
    assert sum(chunks) == D_FF and all(c % LANES == 0 for c in chunks)
    t, d = x1.shape
    tps = seq // tm
    hpt = tm // HALO
    last = t // HALO - 1
    return pl.pallas_call(
        functools.partial(_ffn_kernel, tps=tps, chunks=chunks),
        grid=(t // tm,),
        in_specs=[pl.BlockSpec((tm, d), lambda i: (i, 0)),
                  pl.BlockSpec((HALO, d), lambda i: (jnp.maximum(i * hpt - 1, 0), 0)),
                  pl.BlockSpec((HALO, d), lambda i: (jnp.minimum((i + 1) * hpt, last), 0)),
                  pl.BlockSpec((None, 6, d), lambda i: (i // tps, 0, 0)),
                  _const_spec((1, d)), _const_spec((1, d)),
                  _const_spec(w_up.shape), _const_spec(conv_w.shape), _const_spec(conv_b.shape),
                  _const_spec(w_down.shape)],
        out_specs=pl.BlockSpec((tm, d), lambda i: (i, 0)),
        out_shape=jax.ShapeDtypeStruct((t, d), F32),
        scratch_shapes=[pltpu.VMEM((FFN_SLOTS, tm + HALO, max(chunks)), F32)] * 2
                       + [pltpu.VMEM((d // LANES, tm, LANES), F32)] * 2,
        compiler_params=_params("parallel"),
        name="ffn",
    )(x1, x1, x1, mod3, g_pre, g_post, w_up, conv_w, conv_b, w_down)


def _layer(x, mod3, p, tables, nbr_bias):
    batch, seq, d = x.shape
    x2 = x.reshape(batch * seq, d)
    qkv_a, qkv_b, ga, gb = _inproj(x2, mod3, p["g_mix_pre"], p["w_in"], tables, seq, INPROJ_TM)
    oa = _dilated_attention(qkv_a, batch, seq).reshape(batch * seq, WIDTH)
    ob = _nbr_attention(*qkv_b, nbr_bias, batch, seq)
    x1 = _merge(oa, ob, ga, gb, x2, mod3,
                p["g_mix_post"], p["w_branch_a"], p["w_branch_b"], p["w_out"], seq, MERGE_TM)
    y = _ffn(x1, mod3, p["g_ffn_pre"], p["g_ffn_post"], p["w_up"], p["conv_w"], p["conv_b"],
             p["w_down"], seq, FFN_TM, FFN_CHUNKS)
    return y.reshape(batch, seq, d)


def kernel(x_prompt, x_sample, c_prompt, c_sample, w_ada, b_ada, g_mix_pre, g_mix_post, g_ffn_pre,
           g_ffn_post, w_in, rpb, w_branch_a, w_branch_b, w_out, w_up, conv_w, conv_b, w_down):
    assert w_ada.shape[0] == 1, "single-layer trunk"
    row = lambda a: a[0].reshape(1, -1)
    p = {
        "w_ada": w_ada[0], "b_ada": b_ada[0],
        "g_mix_pre": row(g_mix_pre), "g_mix_post": row(g_mix_post),
        "g_ffn_pre": row(g_ffn_pre), "g_ffn_post": row(g_ffn_post),
        "w_in": w_in[0].astype(BF16),
        "w_branch_a": w_branch_a[0].astype(BF16), "w_branch_b": w_branch_b[0].astype(BF16),
        "w_out": w_out[0].astype(BF16), "w_up": w_up[0].astype(BF16),
        "conv_w": conv_w[0], "conv_b": row(conv_b), "w_down": w_down[0].astype(BF16),
    }
    nbr_bias = _nbr_bias(rpb[0])
    tables = _rope_tables(x_prompt.shape[1])
    n_prompt = x_prompt.shape[0]
    mod = _modulation(jnp.concatenate([c_prompt, c_sample], axis=0), p["w_ada"], p["b_ada"])
    mod3 = mod.reshape(mod.shape[0], 6, D_MODEL)
    y_prompt = _layer(x_prompt, mod3[:n_prompt], p, tables, nbr_bias)
    y_sample = _layer(x_sample, mod3[n_prompt:], p, tables, nbr_bias)
    return (y_prompt, y_sample)
```

```python
import functools

import numpy as np
import jax
import jax.numpy as jnp
from jax import lax
from jax.experimental import pallas as pl
from jax.experimental.pallas import tpu as pltpu

D_MODEL = 1024
HEAD_DIM = 64
N_HEADS = 8
WIDTH = N_HEADS * HEAD_DIM
DILATIONS = (1, 4, 16)
BAND_HALF = 64
ROPE_THETA = 500000.0
ROPE_DIM = HEAD_DIM // 4
GRID_W = 64
NA_ROWS = 8
NA_COLS = 16
D_FF = 2816
EPS = 1e-6
NEG_INF = -1e30
LOG2E = 1.4426950408889634
GELU_C = 0.7978845608028654

LANES = 128
Q_BLOCK = 128
NBR_ROWS_PER_STEP = 16
VMEM_LIMIT = 56 * 1024 * 1024
INPROJ_TM = 512
MERGE_TM = 1024

F32 = jnp.float32
BF16 = jnp.bfloat16


def _params(*sem):
    return pltpu.CompilerParams(dimension_semantics=sem, vmem_limit_bytes=VMEM_LIMIT)


def _const_spec(shape):
    nd = len(shape)
    return pl.BlockSpec(shape, lambda *_: (0,) * nd, pipeline_mode=pl.Buffered(1))


def _rms(x, g):
    return x * lax.rsqrt(jnp.mean(x * x, axis=-1, keepdims=True) + EPS) * g


def _mod_kernel(c_ref, w_ref, b_ref, o_ref):
    c = c_ref[...]
    s = c * jax.nn.sigmoid(c)
    w = w_ref[...]
    s_hi = s.astype(BF16)
    s_lo = (s - s_hi.astype(F32)).astype(BF16)
    w_hi = w.astype(BF16)
    w_lo = (w - w_hi.astype(F32)).astype(BF16)
    acc = jnp.dot(s_hi, w_hi, preferred_element_type=F32)
    acc += jnp.dot(s_hi, w_lo, preferred_element_type=F32)
    acc += jnp.dot(s_lo, w_hi, preferred_element_type=F32)
    o_ref[...] = acc + b_ref[...]


def _modulation(c, w_ada, b_ada):
    b, d = c.shape
    n = w_ada.shape[1]
    return pl.pallas_call(
        _mod_kernel,
        grid=(n // d,),
        in_specs=[_const_spec((b, d)),
                  pl.BlockSpec((d, d), lambda j: (0, j)),
                  pl.BlockSpec((1, d), lambda j: (0, j))],
        out_specs=pl.BlockSpec((b, d), lambda j: (0, j)),
        out_shape=jax.ShapeDtypeStruct((b, n), F32),
        compiler_params=_params("arbitrary"),
        name="mod",
    )(c, w_ada, b_ada.reshape(1, n))


def _inproj_kernel(x_ref, mod_ref, g_ref, w_ref, cos_ref, sa_ref, sb_ref,
                   q1_ref, k1_ref, v1_ref, q4_ref, k4_ref, v4_ref, q16_ref, k16_ref, v16_ref,
                   qb_ref, kb_ref, vb_ref, ga_ref, gb_ref, stage_ref, slab_ref):
    tm = x_ref.shape[0]
    h = _rms(x_ref[...], g_ref[...]) * (1.0 + mod_ref[1:2, :]) + mod_ref[0:1, :]
    hb = h.astype(BF16)

    def proj(c0, width):
        return jnp.dot(hb, w_ref[:, c0:c0 + width], preferred_element_type=F32)

    cos, sa, sb = cos_ref[...], sa_ref[...], sb_ref[...]

    def rope(t):
        outs = []
        for g in range(t.shape[1] // LANES):
            tg = t[:, g * LANES:(g + 1) * LANES]
            outs.append(tg * cos + pltpu.roll(tg, ROPE_DIM // 2, 1) * sa
                        + pltpu.roll(tg, LANES - ROPE_DIM // 2, 1) * sb)
        return jnp.concatenate(outs, axis=1)

    def emit(t, refs):
        ref1, ref4, ref16 = refs
        step = DILATIONS[1]
        assert DILATIONS[2] == step * step
        for g in range(WIDTH // LANES):
            tg = t[:, g * LANES:(g + 1) * LANES]
            ref1[g] = tg.astype(BF16)
            stage_ref[g] = tg
            for r in range(step):
                slab = stage_ref[g, pl.ds(r, tm // step, stride=step), :]
                ref4[g, :, r * LANES:(r + 1) * LANES] = slab.astype(BF16)
                slab_ref[g * step + r] = slab
        for r16 in range(DILATIONS[2]):
            r, u = r16 % step, r16 // step
            for g in range(WIDTH // LANES):
                ref16[g, :, r16 * LANES:(r16 + 1) * LANES] = (
                    slab_ref[g * step + r, pl.ds(u, tm // DILATIONS[2], stride=step), :].astype(BF16))

    scale = HEAD_DIM ** -0.5 * LOG2E

    emit(rope(proj(0, WIDTH)) * scale, (q1_ref, q4_ref, q16_ref))
    emit(rope(proj(WIDTH, WIDTH)), (k1_ref, k4_ref, k16_ref))
    emit(proj(2 * WIDTH, WIDTH), (v1_ref, v4_ref, v16_ref))
    qb_ref[...] = (proj(3 * WIDTH, WIDTH) * scale).astype(BF16)
    kb_ref[...] = proj(4 * WIDTH, WIDTH).astype(BF16)
    vb_ref[...] = proj(5 * WIDTH, WIDTH).astype(BF16)
    ga_ref[...] = jax.nn.sigmoid(proj(6 * WIDTH, D_MODEL)).astype(BF16)
    gb_ref[...] = jax.nn.sigmoid(proj(6 * WIDTH + D_MODEL, D_MODEL)).astype(BF16)


def _rope_tables(seq):
    half = ROPE_DIM // 2
    inv = jnp.power(jnp.float32(ROPE_THETA), -jnp.arange(half, dtype=F32) / half)
    ang = jnp.arange(seq).astype(F32)[:, None] * inv[None, :]
    cos, sin = jnp.cos(ang), jnp.sin(ang)
    ones = jnp.ones((seq, HEAD_DIM - ROPE_DIM), F32)
    zeros = jnp.zeros((seq, HEAD_DIM - ROPE_DIM), F32)
    zh = jnp.zeros((seq, half), F32)
    per_head = lambda *parts: jnp.tile(jnp.concatenate(parts, axis=1), (1, LANES // HEAD_DIM))
    return (per_head(cos, cos, ones),
            per_head(zh, sin, zeros),
            per_head(-sin, zh, zeros))


def _inproj(x2, mod3, g_pre, w_in, tables, seq, tm):
    t, d = x2.shape
    tps = seq // tm
    row = lambda w: pl.BlockSpec((tm, w), lambda i: (i, 0))
    tab = pl.BlockSpec((tm, LANES), lambda i: (i % tps, 0))
    pairs = WIDTH // LANES
    dil_shapes = [jax.ShapeDtypeStruct((t // seq, pairs, seq // dil, dil * LANES), BF16)
                  for dil in DILATIONS]
    dil_specs = [pl.BlockSpec((None, pairs, tm // dil, dil * LANES),
                              lambda i: (i // tps, 0, i % tps, 0)) for dil in DILATIONS]
    outs = pl.pallas_call(
        _inproj_kernel,
        grid=(t // tm,),
        in_specs=[row(d),
                  pl.BlockSpec((None, 6, d), lambda i: (i // tps, 0, 0)),
                  _const_spec((1, d)),
                  _const_spec(w_in.shape),
                  tab, tab, tab],
        out_specs=[s for s in dil_specs for _ in range(3)] + [row(WIDTH)] * 3 + [row(d)] * 2,
        out_shape=[s for s in dil_shapes for _ in range(3)]
                  + [jax.ShapeDtypeStruct((t, WIDTH), BF16)] * 3
                  + [jax.ShapeDtypeStruct((t, d), BF16)] * 2,
        scratch_shapes=[pltpu.VMEM((pairs, tm, LANES), F32),
                        pltpu.VMEM((pairs * DILATIONS[1], tm // DILATIONS[1], LANES), F32)],
        compiler_params=_params("parallel"),
        name="inproj",
    )(x2, mod3, g_pre, w_in, *tables)
    qkv_a = [outs[3 * n:3 * n + 3] for n in range(len(DILATIONS))]
    return qkv_a, outs[9:12], outs[12], outs[13]


def _pair_attention(units, lo_mask, normalise):
    zero = jnp.zeros_like(units[0][0])
    scores = []
    for q2, kwin, _, bias in units:
        qx = jnp.concatenate([jnp.where(lo_mask, q2, zero), jnp.where(lo_mask, zero, q2)], axis=0)
        s = lax.dot_general(qx, kwin, (((1,), (1,)), ((), ())), preferred_element_type=F32)
        scores.append(s + bias)
    maxes = [jnp.max(s, axis=1, keepdims=True) for s in scores]
    probs = [jnp.exp2(s - m).astype(BF16) for s, m in zip(scores, maxes)]
    for p, m, (q2, _, vwin, _) in zip(probs, maxes, units):
        rows = q2.shape[0]
        vaug = jnp.concatenate([vwin, jnp.ones((vwin.shape[0], LANES), BF16)], axis=1)
        r = jnp.dot(p, vaug, preferred_element_type=F32)
        o, den = r[:, :LANES], r[:, LANES:]
        if normalise:
            o = o / den
        pick = lambda t: jnp.where(lo_mask, t[:rows], t[rows:])
        yield pick(o), pick(jnp.broadcast_to(m, den.shape)), pick(den)


def _lo_mask(rows):
    return lax.broadcasted_iota(jnp.int32, (rows, LANES), 1) < HEAD_DIM


def _band_units(bias_ref, q_ref, k_ref, v_ref, dil):
    length = q_ref.shape[0]
    nblk = length // Q_BLOCK
    kw = bias_ref.shape[2]
    units, dests = [], []
    for blk in range(nblk):
        i0 = blk * Q_BLOCK
        ks = min(max(i0 - BAND_HALF, 0), length - kw)
        kind = 0 if blk == 0 else (2 if blk == nblk - 1 else 1)
        bias = bias_ref[min(kind, bias_ref.shape[0] - 1)]
        for r in range(dil):
            c = slice(r * LANES, (r + 1) * LANES)
            units.append((q_ref[i0:i0 + Q_BLOCK, c], k_ref[ks:ks + kw, c], v_ref[ks:ks + kw, c], bias))
            dests.append((blk, r))
    return units, dests


def _dilated_kernel(b1_ref, b4_ref, b16_ref, q1_ref, k1_ref, v1_ref, q4_ref, k4_ref, v4_ref,
                    q16_ref, k16_ref, v16_ref, out_ref, o_s, m_s, d_s, slab_s):
    lo = _lo_mask(Q_BLOCK)
    step = DILATIONS[1]
    assert DILATIONS[2] == step * step
    seq = out_ref.shape[0]

    units, dests = _band_units(b4_ref, q4_ref, k4_ref, v4_ref, step)
    for (blk, r), res in zip(dests, _pair_attention(units, lo, False)):
        rows = pl.ds(blk * (Q_BLOCK * step) + r, Q_BLOCK, stride=step)
        for ref, val in zip((o_s, m_s, d_s), res):
            ref[0, rows, :] = val

    units, dests = _band_units(b16_ref, q16_ref, k16_ref, v16_ref, DILATIONS[2])
    for (blk, r16), res in zip(dests, _pair_attention(units, lo, False)):
        r, u = r16 % step, r16 // step
        rows = pl.ds(blk * (Q_BLOCK * step) + u, Q_BLOCK, stride=step)
        for a, val in enumerate(res):
            slab_s[a * step + r, rows, :] = val
    for a, ref in enumerate((o_s, m_s, d_s)):
        for r in range(step):
            ref[1, pl.ds(r, seq // step, stride=step), :] = slab_s[a * step + r]

    units, dests = _band_units(b1_ref, q1_ref, k1_ref, v1_ref, DILATIONS[0])
    for (blk, _), (o1, m1, d1) in zip(dests, _pair_attention(units, lo, False)):
        rows = slice(blk * Q_BLOCK, (blk + 1) * Q_BLOCK)
        o2, m2, d2 = o_s[0, rows, :], m_s[0, rows, :], d_s[0, rows, :]
        o3, m3, d3 = o_s[1, rows, :], m_s[1, rows, :], d_s[1, rows, :]
        m = jnp.maximum(jnp.maximum(m1, m2), m3)
        e1, e2, e3 = jnp.exp2(m1 - m), jnp.exp2(m2 - m), jnp.exp2(m3 - m)
        out_ref[rows, :] = ((e1 * o1 + e2 * o2 + e3 * o3)
                            / (e1 * d1 + e2 * d2 + e3 * d3)).astype(BF16)


def _band_bias(length):
    kw = min(2 * Q_BLOCK, length)
    qi = np.arange(Q_BLOCK)[:, None]
    kj = np.arange(kw)[None, :]
    offsets = (0,) if length == Q_BLOCK else (0, -BAND_HALF, Q_BLOCK - kw)
    tiles = [np.where(np.abs(kj + off - qi) <= BAND_HALF, 0.0, NEG_INF) for off in offsets]
    return jnp.asarray(np.tile(np.stack(tiles), (1, 2, 1)), F32)


def _dilated_attention(qkv_a, batch, seq):
    pairs = WIDTH // LANES
    biases = [_band_bias(seq // dil) for dil in DILATIONS]
    spec = lambda dil: pl.BlockSpec((None, None, seq // dil, dil * LANES), lambda b, g: (b, g, 0, 0))
    return pl.pallas_call(
        _dilated_kernel,
        grid=(batch, pairs),
        in_specs=[_const_spec(b.shape) for b in biases]
                 + [spec(dil) for dil in DILATIONS for _ in range(3)],
        out_specs=pl.BlockSpec((None, seq, LANES), lambda b, g: (b, 0, g)),
        out_shape=jax.ShapeDtypeStruct((batch, seq, WIDTH), BF16),
        scratch_shapes=[pltpu.VMEM((len(DILATIONS) - 1, seq, LANES), F32)] * 3
                       + [pltpu.VMEM((3 * DILATIONS[1], seq // DILATIONS[1], LANES), F32)],
        compiler_params=_params("parallel", "parallel"),
        name="dilated",
    )(*biases, *[a for qkv in qkv_a for a in qkv])


def _nbr_kernel(bias_ref, q_ref, k_ref, v_ref, o_ref, *, rows):
    lo = _lo_mask(GRID_W)
    nkeys = NA_ROWS * GRID_W

    pairs = WIDTH // LANES
    rps = NBR_ROWS_PER_STEP

    def step(it, carry):
        units, dests = [], []
        for j in range(rps):
            r = it * rps + j
            rs = jnp.clip(r - NA_ROWS // 2, 0, rows - NA_ROWS)
            var = r - rs
            q0 = pl.multiple_of(r * GRID_W, GRID_W)
            k0 = pl.multiple_of(rs * GRID_W, GRID_W)
            for p in range(pairs):
                c = slice(p * LANES, (p + 1) * LANES)
                units.append((q_ref[pl.ds(q0, GRID_W), c], k_ref[pl.ds(k0, nkeys), c],
                              v_ref[pl.ds(k0, nkeys), c], bias_ref[p * NA_ROWS + var]))
                dests.append((q0, c))
        for (q0, c), (o, _, _) in zip(dests, _pair_attention(units, lo, True)):
            o_ref[pl.ds(q0, GRID_W), c] = o.astype(BF16)
        return carry

    lax.fori_loop(0, rows // rps, step, 0)


def _nbr_bias(rpb):
    c = np.arange(GRID_W)[:, None]
    ck = np.arange(GRID_W)[None, :]
    cs = np.clip(c - NA_COLS // 2, 0, GRID_W - NA_COLS)
    ok = (ck >= cs) & (ck < cs + NA_COLS)
    pad = GRID_W - NA_COLS
    rp = jnp.pad(rpb.astype(F32) * LOG2E, ((0, 0), (0, 0), (pad, pad)))
    u = jnp.pad(rp, ((0, 0), (0, 0), (0, 1))).reshape(N_HEADS, -1)
    period = u.shape[1]
    skew = jnp.tile(u, (1, GRID_W))[:, :GRID_W * (period - 1)].reshape(N_HEADS, GRID_W, period - 1)
    skew = jnp.pad(skew, ((0, 0), (0, 0), (0, 1))).reshape(N_HEADS, GRID_W, 2 * NA_ROWS - 1, 2 * GRID_W)
    t = skew[..., GRID_W - 1:2 * GRID_W - 1]
    t = jnp.where(ok[None, :, None, :], t, NEG_INF)
    t = t.reshape(N_HEADS // 2, 2 * GRID_W, 2 * NA_ROWS - 1, GRID_W)
    b = jnp.stack([t[:, :, NA_ROWS - 1 - v:2 * NA_ROWS - 1 - v] for v in range(NA_ROWS)], axis=1)
    return b.reshape(N_HEADS // 2 * NA_ROWS, 2 * GRID_W, NA_ROWS * GRID_W)


def _nbr_attention(q, k, v, bias, batch, seq):
    view = lambda a: a.reshape(batch, seq, WIDTH)
    spec = pl.BlockSpec((None, seq, WIDTH), lambda b: (b, 0, 0))
    o = pl.pallas_call(
        functools.partial(_nbr_kernel, rows=seq // GRID_W),
        grid=(batch,),
        in_specs=[_const_spec(bias.shape), spec, spec, spec],
        out_specs=spec,
        out_shape=jax.ShapeDtypeStruct((batch, seq, WIDTH), BF16),
        compiler_params=_params("parallel"),
        name="nbr",
    )(bias, view(q), view(k), view(v))
    return o.reshape(batch * seq, WIDTH)


def _merge_kernel(oa_ref, ob_ref, ga_ref, gb_ref,
                  x_ref, mod_ref, g_ref, wa_ref, wb_ref, wo_ref, out_ref):
    ya = jnp.dot(oa_ref[...], wa_ref[...], preferred_element_type=F32)
    yb = jnp.dot(ob_ref[...], wb_ref[...], preferred_element_type=F32)
    merged = ga_ref[...].astype(F32) * ya + gb_ref[...].astype(F32) * yb
    z = jnp.dot(merged.astype(BF16), wo_ref[...], preferred_element_type=F32)
    y = x_ref[...] + mod_ref[2:3, :] * _rms(z, g_ref[...])
    tiles, groups, rows, _ = out_ref.shape
    for n in range(tiles):
        for g in range(groups):
            out_ref[n, g] = y[n * rows:(n + 1) * rows, g * LANES:(g + 1) * LANES]


def _merge(oa, ob, ga, gb, x2, mod3, g_post, wa, wb, wo, seq, tm, out_tm):
    t, d = x2.shape
    tps = seq // tm
    assert tm % out_tm == 0
    row = lambda w: pl.BlockSpec((tm, w), lambda i: (i, 0))
    return pl.pallas_call(
        _merge_kernel,
        grid=(t // tm,),
        in_specs=[row(WIDTH)] * 2 + [row(d)] * 3
                 + [pl.BlockSpec((None, 6, d), lambda i: (i // tps, 0, 0)),
                    _const_spec((1, d)), _const_spec(wa.shape), _const_spec(wb.shape),
                    _const_spec(wo.shape)],
        out_specs=pl.BlockSpec((tm // out_tm, d // LANES, out_tm, LANES), lambda i: (i, 0, 0, 0)),
        out_shape=jax.ShapeDtypeStruct((t // out_tm, d // LANES, out_tm, LANES), F32),
        compiler_params=_params("parallel"),
        name="merge",
    )(oa, ob, ga, gb, x2, mod3, g_post, wa, wb, wo)


HALO = 16
FFN_TM = 512
FFN_SLOTS = 3
SLABS = 8
FFN_CHUNKS = (256, 768, 768, 768, 256)


def _ffn_kernel(x_ref, xp_ref, xn_ref, mod_ref, gpre_ref, gpost_ref, wup_ref, cw_ref, cb_ref,
                wdn_ref, out_ref, uv_ref, ug_ref, os_ref, *, tps, chunks):
    tm = x_ref.shape[1]
    slab = tm // SLABS
    lane_groups = range(D_MODEL // LANES)
    i = pl.program_id(0)
    has_prev = (i % tps != 0).astype(F32)
    has_next = (i % tps != tps - 1).astype(F32)
    shift, gain, gpre = mod_ref[3:4, :], 1.0 + mod_ref[4:5, :], gpre_ref[...]

    def prenorm(x):
        return _rms(x, gpre) * gain + shift

    x = jnp.concatenate(
        [jnp.concatenate([x_ref[g, pl.ds(j, slab, stride=SLABS), :] for g in lane_groups], axis=1)
         for j in range(SLABS)], axis=0)
    first = lax.broadcasted_iota(jnp.int32, (HALO, D_MODEL), 0) == 0
    token = lambda ref, r: jnp.concatenate([ref[g, r:r + 1, :] for g in lane_groups], axis=1)
    halo = jnp.where(first, token(xp_ref, HALO - 1), token(xn_ref, 0))
    hext = jnp.concatenate([prenorm(x).astype(BF16),
                            (prenorm(halo) * jnp.where(first, has_prev, has_next)).astype(BF16)], axis=0)
    prev_row, next_row = tm, tm + 1
    starts = [sum(chunks[:j]) for j in range(len(chunks))]

    def up(j):
        c0, w = starts[j], chunks[j]
        uv_ref[j % FFN_SLOTS, :, 0:w] = jnp.dot(hext, wup_ref[:, c0:c0 + w], preferred_element_type=F32)
        ug_ref[j % FFN_SLOTS, :, 0:w] = jnp.dot(hext, wup_ref[:, D_FF + c0:D_FF + c0 + w],
                                        preferred_element_type=F32)

    def activation(j):
        c0, w = starts[j], chunks[j]

        row = lax.broadcasted_iota(jnp.int32, (slab, w), 0)

        def conv(u_ref, col0, gain):
            cols = slice(col0, col0 + w)
            w0, w1, w2, b = (gain * t for t in (cw_ref[0:1, cols], cw_ref[1:2, cols],
                                                cw_ref[2:3, cols], cb_ref[:, cols]))
            rows = lambda r0: u_ref[j % FFN_SLOTS, pl.ds(r0, slab), 0:w]
            out = []
            for s in range(SLABS):
                prev = rows((s - 1) * slab) if s > 0 else jnp.where(
                    row == 0, u_ref[j % FFN_SLOTS, prev_row:prev_row + 1, 0:w], rows((SLABS - 1) * slab - 1))
                nxt = rows((s + 1) * slab) if s < SLABS - 1 else jnp.where(
                    row == slab - 1, u_ref[j % FFN_SLOTS, next_row:next_row + 1, 0:w], rows(1))
                out.append(w0 * prev + w1 * rows(s * slab) + w2 * nxt + b)
            return jnp.concatenate(out, axis=0)

        g = conv(ug_ref, D_FF + c0, 1.0)
        gv = g * conv(uv_ref, c0, 0.5)
        t = jnp.tanh(g * (GELU_C + (GELU_C * 0.044715) * (g * g)))
        return (gv + gv * t).astype(BF16)

    ahead = FFN_SLOTS - 1
    for j in range(min(ahead, len(chunks))):
        up(j)
    acc = jnp.zeros((tm, D_MODEL), F32)
    for j in range(len(chunks)):
        if j + ahead < len(chunks):
            up(j + ahead)
        c0, w = starts[j], chunks[j]
        acc += jnp.dot(activation(j), wdn_ref[c0:c0 + w, :], preferred_element_type=F32)
    y = x + mod_ref[5:6, :] * _rms(acc, gpost_ref[...])
    for s in range(SLABS):
        for g in lane_groups:
            os_ref[g, pl.ds(s, slab, stride=SLABS), :] = y[s * slab:(s + 1) * slab,
                                                           g * LANES:(g + 1) * LANES]
    for g in lane_groups:
        out_ref[:, g * LANES:(g + 1) * LANES] = os_ref[g]


def _ffn(x1, mod3, g_pre, g_post, w_up, conv_w, conv_b, w_down, seq, tm, chunks):
    assert sum(chunks) == D_FF and all(c % LANES == 0 for c in chunks)
    tiles, groups, tile_rows, _ = x1.shape
    assert tile_rows == tm
    t, d = tiles * tm, groups * LANES
    tps = seq // tm
    halo = lambda tile, blk: pl.BlockSpec((None, groups, HALO, LANES), lambda i: (tile(i), 0, blk, 0))
    return pl.pallas_call(
        functools.partial(_ffn_kernel, tps=tps, chunks=chunks),
        grid=(tiles,),
        in_specs=[pl.BlockSpec((None, groups, tm, LANES), lambda i: (i, 0, 0, 0)),
                  halo(lambda i: jnp.maximum(i - 1, 0), tm // HALO - 1),
                  halo(lambda i: jnp.minimum(i + 1, tiles - 1), 0),
                  pl.BlockSpec((None, 6, d), lambda i: (i // tps, 0, 0)),
                  _const_spec((1, d)), _const_spec((1, d)),
                  _const_spec(w_up.shape), _const_spec(conv_w.shape), _const_spec(conv_b.shape),
                  _const_spec(w_down.shape)],
        out_specs=pl.BlockSpec((tm, d), lambda i: (i, 0)),
        out_shape=jax.ShapeDtypeStruct((t, d), F32),
        scratch_shapes=[pltpu.VMEM((FFN_SLOTS, tm + HALO, max(chunks)), F32)] * 2
                       + [pltpu.VMEM((groups, tm, LANES), F32)],
        compiler_params=_params("parallel"),
        name="ffn",
    )(x1, x1, x1, mod3, g_pre, g_post, w_up, conv_w, conv_b, w_down)


def _layer(x, mod3, p, tables, nbr_bias):
    batch, seq, d = x.shape
    x2 = x.reshape(batch * seq, d)
    qkv_a, qkv_b, ga, gb = _inproj(x2, mod3, p["g_mix_pre"], p["w_in"], tables, seq, INPROJ_TM)
    oa = _dilated_attention(qkv_a, batch, seq).reshape(batch * seq, WIDTH)
    ob = _nbr_attention(*qkv_b, nbr_bias, batch, seq)
    x1 = _merge(oa, ob, ga, gb, x2, mod3,
                p["g_mix_post"], p["w_branch_a"], p["w_branch_b"], p["w_out"], seq, MERGE_TM, FFN_TM)
    y = _ffn(x1, mod3, p["g_ffn_pre"], p["g_ffn_post"], p["w_up"], p["conv_w"], p["conv_b"],
             p["w_down"], seq, FFN_TM, FFN_CHUNKS)
    return y.reshape(batch, seq, d)


def kernel(x_prompt, x_sample, c_prompt, c_sample, w_ada, b_ada, g_mix_pre, g_mix_post, g_ffn_pre,
           g_ffn_post, w_in, rpb, w_branch_a, w_branch_b, w_out, w_up, conv_w, conv_b, w_down):
    assert w_ada.shape[0] == 1, "single-layer trunk"
    row = lambda a: a[0].reshape(1, -1)
    p = {
        "w_ada": w_ada[0], "b_ada": b_ada[0],
        "g_mix_pre": row(g_mix_pre), "g_mix_post": row(g_mix_post),
        "g_ffn_pre": row(g_ffn_pre), "g_ffn_post": row(g_ffn_post),
        "w_in": w_in[0].astype(BF16),
        "w_branch_a": w_branch_a[0].astype(BF16), "w_branch_b": w_branch_b[0].astype(BF16),
        "w_out": w_out[0].astype(BF16), "w_up": w_up[0].astype(BF16),
        "conv_w": conv_w[0], "conv_b": row(conv_b), "w_down": w_down[0].astype(BF16),
    }
    nbr_bias = _nbr_bias(rpb[0])
    tables = _rope_tables(x_prompt.shape[1])
    n_prompt = x_prompt.shape[0]
    mod = _modulation(jnp.concatenate([c_prompt, c_sample], axis=0), p["w_ada"], p["b_ada"])
    mod3 = mod.reshape(mod.shape[0], 6, D_MODEL)
    y_prompt = _layer(x_prompt, mod3[:n_prompt], p, tables, nbr_bias)
    y_sample = _layer(x_sample, mod3[n_prompt:], p, tables, nbr_bias)
    return (y_prompt, y_sample)
```

```python
import functools

import numpy as np
import jax
import jax.numpy as jnp
from jax import lax
from jax.experimental import pallas as pl
from jax.experimental.pallas import tpu as pltpu

D_MODEL = 1024
HEAD_DIM = 64
N_HEADS = 8
WIDTH = N_HEADS * HEAD_DIM
DILATIONS = (1, 4, 16)
BAND_HALF = 64
ROPE_THETA = 500000.0
ROPE_DIM = HEAD_DIM // 4
GRID_W = 64
NA_ROWS = 8
NA_COLS = 16
D_FF = 2816
EPS = 1e-6
NEG_INF = -1e30
LOG2E = 1.4426950408889634
GELU_C = 0.7978845608028654
GELU_CUBIC = 0.044715

LANES = 128
Q_BLOCK = 128
NBR_ROWS_PER_STEP = 16
VMEM_LIMIT = 56 * 1024 * 1024
INPROJ_TM = 512
MERGE_TM = 1024

F32 = jnp.float32
BF16 = jnp.bfloat16


def _params(*sem):
    return pltpu.CompilerParams(dimension_semantics=sem, vmem_limit_bytes=VMEM_LIMIT)


def _const_spec(shape):
    nd = len(shape)
    return pl.BlockSpec(shape, lambda *_: (0,) * nd, pipeline_mode=pl.Buffered(1))


def _rms(x, g):
    return x * lax.rsqrt(jnp.mean(x * x, axis=-1, keepdims=True) + EPS) * g


def _mod_kernel(c_ref, w_ref, b_ref, o_ref):
    c = c_ref[...]
    s = c * jax.nn.sigmoid(c)
    w = w_ref[...]
    s_hi = s.astype(BF16)
    s_lo = (s - s_hi.astype(F32)).astype(BF16)
    w_hi = w.astype(BF16)
    w_lo = (w - w_hi.astype(F32)).astype(BF16)
    acc = jnp.dot(s_hi, w_hi, preferred_element_type=F32)
    acc += jnp.dot(s_hi, w_lo, preferred_element_type=F32)
    acc += jnp.dot(s_lo, w_hi, preferred_element_type=F32)
    o_ref[...] = acc + b_ref[...]


def _modulation(c, w_ada, b_ada):
    b, d = c.shape
    n = w_ada.shape[1]
    return pl.pallas_call(
        _mod_kernel,
        grid=(n // d,),
        in_specs=[_const_spec((b, d)),
                  pl.BlockSpec((d, d), lambda j: (0, j)),
                  pl.BlockSpec((1, d), lambda j: (0, j))],
        out_specs=pl.BlockSpec((b, d), lambda j: (0, j)),
        out_shape=jax.ShapeDtypeStruct((b, n), F32),
        compiler_params=_params("arbitrary"),
        name="mod",
    )(c, w_ada, b_ada.reshape(1, n))


def _inproj_kernel(x_ref, mod_ref, g_ref, w_ref, cos_ref, sa_ref, sb_ref,
                   q1_ref, k1_ref, v1_ref, q4_ref, k4_ref, v4_ref, q16_ref, k16_ref, v16_ref,
                   qb_ref, kb_ref, vb_ref, ga_ref, gb_ref, stage_ref, slab_ref):
    tm = x_ref.shape[0]
    h = _rms(x_ref[...], g_ref[...]) * (1.0 + mod_ref[1:2, :]) + mod_ref[0:1, :]
    hb = h.astype(BF16)

    def proj(c0, width):
        return jnp.dot(hb, w_ref[:, c0:c0 + width], preferred_element_type=F32)

    cos, sa, sb = cos_ref[...], sa_ref[...], sb_ref[...]

    def rope(t):
        outs = []
        for g in range(t.shape[1] // LANES):
            tg = t[:, g * LANES:(g + 1) * LANES]
            outs.append(tg * cos + pltpu.roll(tg, ROPE_DIM // 2, 1) * sa
                        + pltpu.roll(tg, LANES - ROPE_DIM // 2, 1) * sb)
        return jnp.concatenate(outs, axis=1)

    def emit(t, refs):
        ref1, ref4, ref16 = refs
        step = DILATIONS[1]
        assert DILATIONS[2] == step * step
        for g in range(WIDTH // LANES):
            tg = t[:, g * LANES:(g + 1) * LANES]
            ref1[g] = tg.astype(BF16)
            stage_ref[g] = tg
            for r in range(step):
                slab = stage_ref[g, pl.ds(r, tm // step, stride=step), :]
                ref4[g, :, r * LANES:(r + 1) * LANES] = slab.astype(BF16)
                slab_ref[g * step + r] = slab
        for r16 in range(DILATIONS[2]):
            r, u = r16 % step, r16 // step
            for g in range(WIDTH // LANES):
                ref16[g, :, r16 * LANES:(r16 + 1) * LANES] = (
                    slab_ref[g * step + r, pl.ds(u, tm // DILATIONS[2], stride=step), :].astype(BF16))

    scale = HEAD_DIM ** -0.5 * LOG2E

    emit(rope(proj(0, WIDTH)) * scale, (q1_ref, q4_ref, q16_ref))
    emit(rope(proj(WIDTH, WIDTH)), (k1_ref, k4_ref, k16_ref))
    emit(proj(2 * WIDTH, WIDTH), (v1_ref, v4_ref, v16_ref))
    qb_ref[...] = (proj(3 * WIDTH, WIDTH) * scale).astype(BF16)
    kb_ref[...] = proj(4 * WIDTH, WIDTH).astype(BF16)
    vb_ref[...] = proj(5 * WIDTH, WIDTH).astype(BF16)
    ga_ref[...] = jax.nn.sigmoid(proj(6 * WIDTH, D_MODEL)).astype(BF16)
    gb_ref[...] = jax.nn.sigmoid(proj(6 * WIDTH + D_MODEL, D_MODEL)).astype(BF16)


def _rope_tables(seq):
    half = ROPE_DIM // 2
    inv = jnp.power(jnp.float32(ROPE_THETA), -jnp.arange(half, dtype=F32) / half)
    ang = jnp.arange(seq).astype(F32)[:, None] * inv[None, :]
    cos, sin = jnp.cos(ang), jnp.sin(ang)
    ones = jnp.ones((seq, HEAD_DIM - ROPE_DIM), F32)
    zeros = jnp.zeros((seq, HEAD_DIM - ROPE_DIM), F32)
    zh = jnp.zeros((seq, half), F32)
    per_head = lambda *parts: jnp.tile(jnp.concatenate(parts, axis=1), (1, LANES // HEAD_DIM))
    return (per_head(cos, cos, ones),
            per_head(zh, sin, zeros),
            per_head(-sin, zh, zeros))


def _inproj(x2, mod3, g_pre, w_in, tables, seq, tm):
    t, d = x2.shape
    tps = seq // tm
    row = lambda w: pl.BlockSpec((tm, w), lambda i: (i, 0))
    tab = pl.BlockSpec((tm, LANES), lambda i: (i % tps, 0))
    pairs = WIDTH // LANES
    dil_shapes = [jax.ShapeDtypeStruct((t // seq, pairs, seq // dil, dil * LANES), BF16)
                  for dil in DILATIONS]
    dil_specs = [pl.BlockSpec((None, pairs, tm // dil, dil * LANES),
                              lambda i: (i // tps, 0, i % tps, 0)) for dil in DILATIONS]
    outs = pl.pallas_call(
        _inproj_kernel,
        grid=(t // tm,),
        in_specs=[row(d),
                  pl.BlockSpec((None, 6, d), lambda i: (i // tps, 0, 0)),
                  _const_spec((1, d)),
                  _const_spec(w_in.shape),
                  tab, tab, tab],
        out_specs=[s for s in dil_specs for _ in range(3)] + [row(WIDTH)] * 3 + [row(d)] * 2,
        out_shape=[s for s in dil_shapes for _ in range(3)]
                  + [jax.ShapeDtypeStruct((t, WIDTH), BF16)] * 3
                  + [jax.ShapeDtypeStruct((t, d), BF16)] * 2,
        scratch_shapes=[pltpu.VMEM((pairs, tm, LANES), F32),
                        pltpu.VMEM((pairs * DILATIONS[1], tm // DILATIONS[1], LANES), F32)],
        compiler_params=_params("parallel"),
        name="inproj",
    )(x2, mod3, g_pre, w_in, *tables)
    qkv_a = [outs[3 * n:3 * n + 3] for n in range(len(DILATIONS))]
    return qkv_a, outs[9:12], outs[12], outs[13]


def _pair_attention(units, lo_mask, normalise):
    zero = jnp.zeros_like(units[0][0])
    scores = []
    for q2, kwin, _, bias in units:
        qx = jnp.concatenate([jnp.where(lo_mask, q2, zero), jnp.where(lo_mask, zero, q2)], axis=0)
        s = lax.dot_general(qx, kwin, (((1,), (1,)), ((), ())), preferred_element_type=F32)
        scores.append(s + bias)
    maxes = [jnp.max(s, axis=1, keepdims=True) for s in scores]
    probs = [jnp.exp2(s - m).astype(BF16) for s, m in zip(scores, maxes)]
    for p, m, (q2, _, vwin, _) in zip(probs, maxes, units):
        rows = q2.shape[0]
        vaug = jnp.concatenate([vwin, jnp.ones((vwin.shape[0], LANES), BF16)], axis=1)
        r = jnp.dot(p, vaug, preferred_element_type=F32)
        o, den = r[:, :LANES], r[:, LANES:]
        if normalise:
            o = o / den
        pick = lambda t: jnp.where(lo_mask, t[:rows], t[rows:])
        yield pick(o), pick(jnp.broadcast_to(m, den.shape)), pick(den)


def _lo_mask(rows):
    return lax.broadcasted_iota(jnp.int32, (rows, LANES), 1) < HEAD_DIM


def _band_units(bias_ref, q_ref, k_ref, v_ref, dil):
    length = q_ref.shape[0]
    nblk = length // Q_BLOCK
    kw = bias_ref.shape[2]
    units, dests = [], []
    for blk in range(nblk):
        i0 = blk * Q_BLOCK
        ks = min(max(i0 - BAND_HALF, 0), length - kw)
        kind = 0 if blk == 0 else (2 if blk == nblk - 1 else 1)
        bias = bias_ref[min(kind, bias_ref.shape[0] - 1)]
        for r in range(dil):
            c = slice(r * LANES, (r + 1) * LANES)
            units.append((q_ref[i0:i0 + Q_BLOCK, c], k_ref[ks:ks + kw, c], v_ref[ks:ks + kw, c], bias))
            dests.append((blk, r))
    return units, dests


def _dilated_kernel(b1_ref, b4_ref, b16_ref, q1_ref, k1_ref, v1_ref, q4_ref, k4_ref, v4_ref,
                    q16_ref, k16_ref, v16_ref, out_ref, o_s, m_s, d_s, slab_s):
    lo = _lo_mask(Q_BLOCK)
    step = DILATIONS[1]
    assert DILATIONS[2] == step * step
    seq = out_ref.shape[0]

    units, dests = _band_units(b4_ref, q4_ref, k4_ref, v4_ref, step)
    for (blk, r), res in zip(dests, _pair_attention(units, lo, False)):
        rows = pl.ds(blk * (Q_BLOCK * step) + r, Q_BLOCK, stride=step)
        for ref, val in zip((o_s, m_s, d_s), res):
            ref[0, rows, :] = val

    units, dests = _band_units(b16_ref, q16_ref, k16_ref, v16_ref, DILATIONS[2])
    for (blk, r16), res in zip(dests, _pair_attention(units, lo, False)):
        r, u = r16 % step, r16 // step
        rows = pl.ds(blk * (Q_BLOCK * step) + u, Q_BLOCK, stride=step)
        for a, val in enumerate(res):
            slab_s[a * step + r, rows, :] = val
    for a, ref in enumerate((o_s, m_s, d_s)):
        for r in range(step):
            ref[1, pl.ds(r, seq // step, stride=step), :] = slab_s[a * step + r]

    units, dests = _band_units(b1_ref, q1_ref, k1_ref, v1_ref, DILATIONS[0])
    for (blk, _), (o1, m1, d1) in zip(dests, _pair_attention(units, lo, False)):
        rows = slice(blk * Q_BLOCK, (blk + 1) * Q_BLOCK)
        o2, m2, d2 = o_s[0, rows, :], m_s[0, rows, :], d_s[0, rows, :]
        o3, m3, d3 = o_s[1, rows, :], m_s[1, rows, :], d_s[1, rows, :]
        m = jnp.maximum(jnp.maximum(m1, m2), m3)
        e1, e2, e3 = jnp.exp2(m1 - m), jnp.exp2(m2 - m), jnp.exp2(m3 - m)
        out_ref[rows, :] = ((e1 * o1 + e2 * o2 + e3 * o3)
                            / (e1 * d1 + e2 * d2 + e3 * d3)).astype(BF16)


def _band_bias(length):
    kw = min(2 * Q_BLOCK, length)
    qi = np.arange(Q_BLOCK)[:, None]
    kj = np.arange(kw)[None, :]
    offsets = (0,) if length == Q_BLOCK else (0, -BAND_HALF, Q_BLOCK - kw)
    tiles = [np.where(np.abs(kj + off - qi) <= BAND_HALF, 0.0, NEG_INF) for off in offsets]
    return jnp.asarray(np.tile(np.stack(tiles), (1, 2, 1)), F32)


def _dilated_attention(qkv_a, batch, seq):
    pairs = WIDTH // LANES
    biases = [_band_bias(seq // dil) for dil in DILATIONS]
    spec = lambda dil: pl.BlockSpec((None, None, seq // dil, dil * LANES), lambda b, g: (b, g, 0, 0))
    return pl.pallas_call(
        _dilated_kernel,
        grid=(batch, pairs),
        in_specs=[_const_spec(b.shape) for b in biases]
                 + [spec(dil) for dil in DILATIONS for _ in range(3)],
        out_specs=pl.BlockSpec((None, seq, LANES), lambda b, g: (b, 0, g)),
        out_shape=jax.ShapeDtypeStruct((batch, seq, WIDTH), BF16),
        scratch_shapes=[pltpu.VMEM((len(DILATIONS) - 1, seq, LANES), F32)] * 3
                       + [pltpu.VMEM((3 * DILATIONS[1], seq // DILATIONS[1], LANES), F32)],
        compiler_params=_params("parallel", "parallel"),
        name="dilated",
    )(*biases, *[a for qkv in qkv_a for a in qkv])


def _nbr_kernel(bias_ref, q_ref, k_ref, v_ref, o_ref, *, rows):
    lo = _lo_mask(GRID_W)
    nkeys = NA_ROWS * GRID_W
    pairs = WIDTH // LANES
    rps = NBR_ROWS_PER_STEP

    def step(it, carry):
        units, dests = [], []
        for j in range(rps):
            r = it * rps + j
            rs = jnp.clip(r - NA_ROWS // 2, 0, rows - NA_ROWS)
            var = r - rs
            q0 = pl.multiple_of(r * GRID_W, GRID_W)
            k0 = pl.multiple_of(rs * GRID_W, GRID_W)
            for p in range(pairs):
                c = slice(p * LANES, (p + 1) * LANES)
                units.append((q_ref[pl.ds(q0, GRID_W), c], k_ref[pl.ds(k0, nkeys), c],
                              v_ref[pl.ds(k0, nkeys), c], bias_ref[p * NA_ROWS + var]))
                dests.append((q0, c))
        for (q0, c), (o, _, _) in zip(dests, _pair_attention(units, lo, True)):
            o_ref[pl.ds(q0, GRID_W), c] = o.astype(BF16)
        return carry

    lax.fori_loop(0, rows // rps, step, 0)


def _nbr_bias(rpb):
    c = np.arange(GRID_W)[:, None]
    ck = np.arange(GRID_W)[None, :]
    cs = np.clip(c - NA_COLS // 2, 0, GRID_W - NA_COLS)
    ok = (ck >= cs) & (ck < cs + NA_COLS)
    pad = GRID_W - NA_COLS
    rp = jnp.pad(rpb.astype(F32) * LOG2E, ((0, 0), (0, 0), (pad, pad)))
    u = jnp.pad(rp, ((0, 0), (0, 0), (0, 1))).reshape(N_HEADS, -1)
    period = u.shape[1]
    skew = jnp.tile(u, (1, GRID_W))[:, :GRID_W * (period - 1)].reshape(N_HEADS, GRID_W, period - 1)
    skew = jnp.pad(skew, ((0, 0), (0, 0), (0, 1))).reshape(N_HEADS, GRID_W, 2 * NA_ROWS - 1, 2 * GRID_W)
    t = skew[..., GRID_W - 1:2 * GRID_W - 1]
    t = jnp.where(ok[None, :, None, :], t, NEG_INF)
    t = t.reshape(N_HEADS // 2, 2 * GRID_W, 2 * NA_ROWS - 1, GRID_W)
    b = jnp.stack([t[:, :, NA_ROWS - 1 - v:2 * NA_ROWS - 1 - v] for v in range(NA_ROWS)], axis=1)
    return b.reshape(N_HEADS // 2 * NA_ROWS, 2 * GRID_W, NA_ROWS * GRID_W)


def _nbr_attention(q, k, v, bias, batch, seq):
    view = lambda a: a.reshape(batch, seq, WIDTH)
    spec = pl.BlockSpec((None, seq, WIDTH), lambda b: (b, 0, 0))
    o = pl.pallas_call(
        functools.partial(_nbr_kernel, rows=seq // GRID_W),
        grid=(batch,),
        in_specs=[_const_spec(bias.shape), spec, spec, spec],
        out_specs=spec,
        out_shape=jax.ShapeDtypeStruct((batch, seq, WIDTH), BF16),
        compiler_params=_params("parallel"),
        name="nbr",
    )(bias, view(q), view(k), view(v))
    return o.reshape(batch * seq, WIDTH)


def _merge_kernel(oa_ref, ob_ref, ga_ref, gb_ref,
                  x_ref, mod_ref, g_ref, wa_ref, wb_ref, wo_ref, out_ref):
    ya = jnp.dot(oa_ref[...], wa_ref[...], preferred_element_type=F32)
    yb = jnp.dot(ob_ref[...], wb_ref[...], preferred_element_type=F32)
    merged = ga_ref[...].astype(F32) * ya + gb_ref[...].astype(F32) * yb
    z = jnp.dot(merged.astype(BF16), wo_ref[...], preferred_element_type=F32)
    y = x_ref[...] + mod_ref[2:3, :] * _rms(z, g_ref[...])
    tiles, groups, rows, _ = out_ref.shape
    for n in range(tiles):
        for g in range(groups):
            out_ref[n, g] = y[n * rows:(n + 1) * rows, g * LANES:(g + 1) * LANES]


def _merge(oa, ob, ga, gb, x2, mod3, g_post, wa, wb, wo, seq, tm, out_tm):
    t, d = x2.shape
    tps = seq // tm
    assert tm % out_tm == 0
    row = lambda w: pl.BlockSpec((tm, w), lambda i: (i, 0))
    return pl.pallas_call(
        _merge_kernel,
        grid=(t // tm,),
        in_specs=[row(WIDTH)] * 2 + [row(d)] * 3
                 + [pl.BlockSpec((None, 6, d), lambda i: (i // tps, 0, 0)),
                    _const_spec((1, d)), _const_spec(wa.shape), _const_spec(wb.shape),
                    _const_spec(wo.shape)],
        out_specs=pl.BlockSpec((tm // out_tm, d // LANES, out_tm, LANES), lambda i: (i, 0, 0, 0)),
        out_shape=jax.ShapeDtypeStruct((t // out_tm, d // LANES, out_tm, LANES), F32),
        compiler_params=_params("parallel"),
        name="merge",
    )(oa, ob, ga, gb, x2, mod3, g_post, wa, wb, wo)


HALO = 16
FFN_TM = 512
FFN_SLOTS = 4
SLABS = 8
FFN_CHUNKS = (256, 768, 768, 768, 256)


def _ffn_kernel(x_ref, xp_ref, xn_ref, mod_ref, gpre_ref, gpost_ref, wup_ref, cw_ref, cb_ref,
                wdn_ref, out_ref, uv_ref, ug_ref, os_ref, *, tps, chunks):
    tm = x_ref.shape[1]
    slab = tm // SLABS
    lane_groups = range(D_MODEL // LANES)
    i = pl.program_id(0)
    has_prev = (i % tps != 0).astype(F32)
    has_next = (i % tps != tps - 1).astype(F32)
    shift, gain, gpre = mod_ref[3:4, :], 1.0 + mod_ref[4:5, :], gpre_ref[...]

    def prenorm(x):
        return _rms(x, gpre) * gain + shift

    x = jnp.concatenate(
        [jnp.concatenate([x_ref[g, pl.ds(j, slab, stride=SLABS), :] for g in lane_groups], axis=1)
         for j in range(SLABS)], axis=0)
    first = lax.broadcasted_iota(jnp.int32, (HALO, D_MODEL), 0) == 0
    token = lambda ref, r: jnp.concatenate([ref[g, r:r + 1, :] for g in lane_groups], axis=1)
    halo = jnp.where(first, token(xp_ref, HALO - 1), token(xn_ref, 0))
    hext = jnp.concatenate([prenorm(x).astype(BF16),
                            (prenorm(halo) * jnp.where(first, has_prev, has_next)).astype(BF16)], axis=0)
    prev_row, next_row = tm, tm + 1
    starts = [sum(chunks[:j]) for j in range(len(chunks))]

    def up(j):
        c0, w = starts[j], chunks[j]
        uv_ref[j % FFN_SLOTS, :, 0:w] = jnp.dot(hext, wup_ref[:, c0:c0 + w], preferred_element_type=F32)
        ug_ref[j % FFN_SLOTS, :, 0:w] = jnp.dot(hext, wup_ref[:, D_FF + c0:D_FF + c0 + w],
                                        preferred_element_type=F32)

    def activation(j):
        c0, w = starts[j], chunks[j]

        row = lax.broadcasted_iota(jnp.int32, (slab, w), 0)

        def conv(u_ref, col0, gain):
            cols = slice(col0, col0 + w)
            w0, w1, w2, b = (gain * t for t in (cw_ref[0:1, cols], cw_ref[1:2, cols],
                                                cw_ref[2:3, cols], cb_ref[:, cols]))
            rows = lambda r0: u_ref[j % FFN_SLOTS, pl.ds(r0, slab), 0:w]
            out = []
            for s in range(SLABS):
                prev = rows((s - 1) * slab) if s > 0 else jnp.where(
                    row == 0, u_ref[j % FFN_SLOTS, prev_row:prev_row + 1, 0:w], rows((SLABS - 1) * slab - 1))
                nxt = rows((s + 1) * slab) if s < SLABS - 1 else jnp.where(
                    row == slab - 1, u_ref[j % FFN_SLOTS, next_row:next_row + 1, 0:w], rows(1))
                out.append(w0 * prev + w1 * rows(s * slab) + w2 * nxt + b)
            return jnp.concatenate(out, axis=0)

        g = conv(ug_ref, D_FF + c0, 1.0)
        gv = g * conv(uv_ref, c0, 0.5)
        t = jnp.tanh(g * (GELU_C + (GELU_C * GELU_CUBIC) * (g * g)))
        return (gv + gv * t).astype(BF16)

    ahead = FFN_SLOTS - 1
    for j in range(min(ahead, len(chunks))):
        up(j)
    acc = jnp.zeros((tm, D_MODEL), F32)
    for j in range(len(chunks)):
        if j + ahead < len(chunks):
            up(j + ahead)
        c0, w = starts[j], chunks[j]
        acc += jnp.dot(activation(j), wdn_ref[c0:c0 + w, :], preferred_element_type=F32)
    y = x + mod_ref[5:6, :] * _rms(acc, gpost_ref[...])
    for s in range(SLABS):
        for g in lane_groups:
            os_ref[g, pl.ds(s, slab, stride=SLABS), :] = y[s * slab:(s + 1) * slab,
                                                           g * LANES:(g + 1) * LANES]
    for g in lane_groups:
        out_ref[:, g * LANES:(g + 1) * LANES] = os_ref[g]


def _ffn(x1, mod3, g_pre, g_post, w_up, conv_w, conv_b, w_down, seq, tm, chunks):
    assert sum(chunks) == D_FF and all(c % LANES == 0 for c in chunks)
    tiles, groups, tile_rows, _ = x1.shape
    assert tile_rows == tm
    t, d = tiles * tm, groups * LANES
    tps = seq // tm
    halo = lambda tile, blk: pl.BlockSpec((None, groups, HALO, LANES), lambda i: (tile(i), 0, blk, 0))
    return pl.pallas_call(
        functools.partial(_ffn_kernel, tps=tps, chunks=chunks),
        grid=(tiles,),
        in_specs=[pl.BlockSpec((None, groups, tm, LANES), lambda i: (i, 0, 0, 0)),
                  halo(lambda i: jnp.maximum(i - 1, 0), tm // HALO - 1),
                  halo(lambda i: jnp.minimum(i + 1, tiles - 1), 0),
                  pl.BlockSpec((None, 6, d), lambda i: (i // tps, 0, 0)),
                  _const_spec((1, d)), _const_spec((1, d)),
                  _const_spec(w_up.shape), _const_spec(conv_w.shape), _const_spec(conv_b.shape),
                  _const_spec(w_down.shape)],
        out_specs=pl.BlockSpec((tm, d), lambda i: (i, 0)),
        out_shape=jax.ShapeDtypeStruct((t, d), F32),
        scratch_shapes=[pltpu.VMEM((FFN_SLOTS, tm + HALO, max(chunks)), F32)] * 2
                       + [pltpu.VMEM((groups, tm, LANES), F32)],
        compiler_params=_params("parallel"),
        name="ffn",
    )(x1, x1, x1, mod3, g_pre, g_post, w_up, conv_w, conv_b, w_down)


def _layer(x, mod3, p, tables, nbr_bias):
    batch, seq, d = x.shape
    x2 = x.reshape(batch * seq, d)
    qkv_a, qkv_b, ga, gb = _inproj(x2, mod3, p["g_mix_pre"], p["w_in"], tables, seq, INPROJ_TM)
    oa = _dilated_attention(qkv_a, batch, seq).reshape(batch * seq, WIDTH)
    ob = _nbr_attention(*qkv_b, nbr_bias, batch, seq)
    x1 = _merge(oa, ob, ga, gb, x2, mod3,
                p["g_mix_post"], p["w_branch_a"], p["w_branch_b"], p["w_out"], seq, MERGE_TM, FFN_TM)
    y = _ffn(x1, mod3, p["g_ffn_pre"], p["g_ffn_post"], p["w_up"], p["conv_w"], p["conv_b"],
             p["w_down"], seq, FFN_TM, FFN_CHUNKS)
    return y.reshape(batch, seq, d)


def kernel(x_prompt, x_sample, c_prompt, c_sample, w_ada, b_ada, g_mix_pre, g_mix_post, g_ffn_pre,
           g_ffn_post, w_in, rpb, w_branch_a, w_branch_b, w_out, w_up, conv_w, conv_b, w_down):
    assert w_ada.shape[0] == 1, "single-layer trunk"
    row = lambda a: a[0].reshape(1, -1)
    p = {
        "w_ada": w_ada[0], "b_ada": b_ada[0],
        "g_mix_pre": row(g_mix_pre), "g_mix_post": row(g_mix_post),
        "g_ffn_pre": row(g_ffn_pre), "g_ffn_post": row(g_ffn_post),
        "w_in": w_in[0].astype(BF16),
        "w_branch_a": w_branch_a[0].astype(BF16), "w_branch_b": w_branch_b[0].astype(BF16),
        "w_out": w_out[0].astype(BF16), "w_up": w_up[0].astype(BF16),
        "conv_w": conv_w[0], "conv_b": row(conv_b), "w_down": w_down[0].astype(BF16),
    }
    nbr_bias = _nbr_bias(rpb[0])
    tables = _rope_tables(x_prompt.shape[1])
    n_prompt = x_prompt.shape[0]
    mod = _modulation(jnp.concatenate([c_prompt, c_sample], axis=0), p["w_ada"], p["b_ada"])
    mod3 = mod.reshape(mod.shape[0], 6, D_MODEL)
    y_prompt = _layer(x_prompt, mod3[:n_prompt], p, tables, nbr_bias)
    y_sample = _layer(x_sample, mod3[n_prompt:], p, tables, nbr_bias)
    return (y_prompt, y_sample)
```

```python
import functools

import numpy as np
import jax
import jax.numpy as jnp
from jax import lax
from jax.experimental import pallas as pl
from jax.experimental.pallas import tpu as pltpu

D_MODEL = 1024
HEAD_DIM = 64
N_HEADS = 8
WIDTH = N_HEADS * HEAD_DIM
DILATIONS = (1, 4, 16)
BAND_HALF = 64
ROPE_THETA = 500000.0
ROPE_DIM = HEAD_DIM // 4
GRID_W = 64
NA_ROWS = 8
NA_COLS = 16
D_FF = 2816
EPS = 1e-6
NEG_INF = -1e30
LOG2E = 1.4426950408889634
GELU_C = 0.7978845608028654
GELU_CUBIC = 0.044715

LANES = 128
Q_BLOCK = 128
NBR_ROWS_PER_STEP = 16
VMEM_LIMIT = 56 * 1024 * 1024
INPROJ_TM = 1024
MERGE_TM = 1024

F32 = jnp.float32
BF16 = jnp.bfloat16


def _params(*sem):
    return pltpu.CompilerParams(dimension_semantics=sem, vmem_limit_bytes=VMEM_LIMIT)


def _const_spec(shape):
    nd = len(shape)
    return pl.BlockSpec(shape, lambda *_: (0,) * nd, pipeline_mode=pl.Buffered(1))


def _rms(x, g):
    return x * lax.rsqrt(jnp.mean(x * x, axis=-1, keepdims=True) + EPS) * g


def _mod_kernel(c_ref, w_ref, b_ref, o_ref):
    c = c_ref[...]
    s = c * jax.nn.sigmoid(c)
    w = w_ref[...]
    s_hi = s.astype(BF16)
    s_lo = (s - s_hi.astype(F32)).astype(BF16)
    w_hi = w.astype(BF16)
    w_lo = (w - w_hi.astype(F32)).astype(BF16)
    acc = jnp.dot(s_hi, w_hi, preferred_element_type=F32)
    acc += jnp.dot(s_hi, w_lo, preferred_element_type=F32)
    acc += jnp.dot(s_lo, w_hi, preferred_element_type=F32)
    o_ref[...] = acc + b_ref[...]


def _modulation(c, w_ada, b_ada):
    b, d = c.shape
    n = w_ada.shape[1]
    return pl.pallas_call(
        _mod_kernel,
        grid=(n // d,),
        in_specs=[_const_spec((b, d)),
                  pl.BlockSpec((d, d), lambda j: (0, j)),
                  pl.BlockSpec((1, d), lambda j: (0, j))],
        out_specs=pl.BlockSpec((b, d), lambda j: (0, j)),
        out_shape=jax.ShapeDtypeStruct((b, n), F32),
        compiler_params=_params("arbitrary"),
        name="mod",
    )(c, w_ada, b_ada.reshape(1, n))


def _gates_kernel(x_ref, mod_ref, g_ref, w_ref, ga_ref, gb_ref):
    h = _rms(x_ref[...], g_ref[...]) * (1.0 + mod_ref[1:2, :]) + mod_ref[0:1, :]
    hb = h.astype(BF16)
    for c0, ref in ((0, ga_ref), (D_MODEL, gb_ref)):
        ref[...] = jax.nn.sigmoid(jnp.dot(hb, w_ref[:, c0:c0 + D_MODEL],
                                          preferred_element_type=F32)).astype(BF16)


def _inproj_kernel(x_ref, mod_ref, g_ref, w_ref, cos_ref, sa_ref, sb_ref,
                   q1_ref, k1_ref, v1_ref, q4_ref, k4_ref, v4_ref, q16_ref, k16_ref, v16_ref,
                   qb_ref, kb_ref, vb_ref, stage_ref, slab_ref):
    tm = x_ref.shape[0]
    h = _rms(x_ref[...], g_ref[...]) * (1.0 + mod_ref[1:2, :]) + mod_ref[0:1, :]
    hb = h.astype(BF16)

    def proj(c0, width):
        return jnp.dot(hb, w_ref[:, c0:c0 + width], preferred_element_type=F32)

    cos, sa, sb = cos_ref[...], sa_ref[...], sb_ref[...]

    def rope(t):
        outs = []
        for g in range(t.shape[1] // LANES):
            tg = t[:, g * LANES:(g + 1) * LANES]
            outs.append(tg * cos + pltpu.roll(tg, ROPE_DIM // 2, 1) * sa
                        + pltpu.roll(tg, LANES - ROPE_DIM // 2, 1) * sb)
        return jnp.concatenate(outs, axis=1)

    def emit(t, refs):
        ref1, ref4, ref16 = refs
        step = DILATIONS[1]
        assert DILATIONS[2] == step * step
        for g in range(WIDTH // LANES):
            tg = t[:, g * LANES:(g + 1) * LANES]
            ref1[g] = tg.astype(BF16)
            stage_ref[g] = tg
            for r in range(step):
                slab = stage_ref[g, pl.ds(r, tm // step, stride=step), :]
                ref4[g, :, r * LANES:(r + 1) * LANES] = slab.astype(BF16)
                slab_ref[g * step + r] = slab
        for r16 in range(DILATIONS[2]):
            r, u = r16 % step, r16 // step
            for g in range(WIDTH // LANES):
                ref16[g, :, r16 * LANES:(r16 + 1) * LANES] = (
                    slab_ref[g * step + r, pl.ds(u, tm // DILATIONS[2], stride=step), :].astype(BF16))

    scale = HEAD_DIM ** -0.5 * LOG2E

    emit(rope(proj(0, WIDTH)) * scale, (q1_ref, q4_ref, q16_ref))
    emit(rope(proj(WIDTH, WIDTH)), (k1_ref, k4_ref, k16_ref))
    emit(proj(2 * WIDTH, WIDTH), (v1_ref, v4_ref, v16_ref))
    qb_ref[...] = (proj(3 * WIDTH, WIDTH) * scale).astype(BF16)
    kb_ref[...] = proj(4 * WIDTH, WIDTH).astype(BF16)
    vb_ref[...] = proj(5 * WIDTH, WIDTH).astype(BF16)


def _rope_tables(seq):
    half = ROPE_DIM // 2
    inv = jnp.power(jnp.float32(ROPE_THETA), -jnp.arange(half, dtype=F32) / half)
    ang = jnp.arange(seq).astype(F32)[:, None] * inv[None, :]
    cos, sin = jnp.cos(ang), jnp.sin(ang)
    ones = jnp.ones((seq, HEAD_DIM - ROPE_DIM), F32)
    zeros = jnp.zeros((seq, HEAD_DIM - ROPE_DIM), F32)
    zh = jnp.zeros((seq, half), F32)
    per_head = lambda *parts: jnp.tile(jnp.concatenate(parts, axis=1), (1, LANES // HEAD_DIM))
    return (per_head(cos, cos, ones),
            per_head(zh, sin, zeros),
            per_head(-sin, zh, zeros))


def _inproj(x2, mod3, g_pre, w_in, tables, seq, tm):
    t, d = x2.shape
    tps = seq // tm
    row = lambda w: pl.BlockSpec((tm, w), lambda i: (i, 0))
    tab = pl.BlockSpec((tm, LANES), lambda i: (i % tps, 0))
    pairs = WIDTH // LANES
    dil_shapes = [jax.ShapeDtypeStruct((t // seq, pairs, seq // dil, dil * LANES), BF16)
                  for dil in DILATIONS]
    dil_specs = [pl.BlockSpec((None, pairs, tm // dil, dil * LANES),
                              lambda i: (i // tps, 0, i % tps, 0)) for dil in DILATIONS]
    w_qkv, w_gates = w_in
    common = [row(d), pl.BlockSpec((None, 6, d), lambda i: (i // tps, 0, 0)), _const_spec((1, d))]
    outs = pl.pallas_call(
        _inproj_kernel,
        grid=(t // tm,),
        in_specs=common + [_const_spec(w_qkv.shape), tab, tab, tab],
        out_specs=[s for s in dil_specs for _ in range(3)] + [row(WIDTH)] * 3,
        out_shape=[s for s in dil_shapes for _ in range(3)]
                  + [jax.ShapeDtypeStruct((t, WIDTH), BF16)] * 3,
        scratch_shapes=[pltpu.VMEM((pairs, tm, LANES), F32),
                        pltpu.VMEM((pairs * DILATIONS[1], tm // DILATIONS[1], LANES), F32)],
        compiler_params=_params("parallel"),
        name="inproj",
    )(x2, mod3, g_pre, w_qkv, *tables)
    ga, gb = pl.pallas_call(
        _gates_kernel,
        grid=(t // tm,),
        in_specs=common + [_const_spec(w_gates.shape)],
        out_specs=[row(d)] * 2,
        out_shape=[jax.ShapeDtypeStruct((t, d), BF16)] * 2,
        compiler_params=_params("parallel"),
        name="gates",
    )(x2, mod3, g_pre, w_gates)
    qkv_a = [outs[3 * n:3 * n + 3] for n in range(len(DILATIONS))]
    return qkv_a, outs[9:12], ga, gb


def _pair_attention(units, lo_mask, normalise):
    zero = jnp.zeros_like(units[0][0])
    scores = []
    for q2, kwin, _, bias in units:
        qx = jnp.concatenate([jnp.where(lo_mask, q2, zero), jnp.where(lo_mask, zero, q2)], axis=0)
        s = lax.dot_general(qx, kwin, (((1,), (1,)), ((), ())), preferred_element_type=F32)
        scores.append(s + bias)
    maxes = [jnp.max(s, axis=1, keepdims=True) for s in scores]
    probs = [jnp.exp2(s - m).astype(BF16) for s, m in zip(scores, maxes)]
    for p, m, (q2, _, vwin, _) in zip(probs, maxes, units):
        rows = q2.shape[0]
        vaug = jnp.concatenate([vwin, jnp.ones((vwin.shape[0], LANES), BF16)], axis=1)
        r = jnp.dot(p, vaug, preferred_element_type=F32)
        o, den = r[:, :LANES], r[:, LANES:]
        if normalise:
            o = o / den
        pick = lambda t: jnp.where(lo_mask, t[:rows], t[rows:])
        yield pick(o), pick(jnp.broadcast_to(m, den.shape)), pick(den)


def _lo_mask(rows):
    return lax.broadcasted_iota(jnp.int32, (rows, LANES), 1) < HEAD_DIM


def _band_units(bias_ref, q_ref, k_ref, v_ref, dil):
    length = q_ref.shape[0]
    nblk = length // Q_BLOCK
    kw = bias_ref.shape[2]
    units, dests = [], []
    for blk in range(nblk):
        i0 = blk * Q_BLOCK
        ks = min(max(i0 - BAND_HALF, 0), length - kw)
        kind = 0 if blk == 0 else (2 if blk == nblk - 1 else 1)
        bias = bias_ref[min(kind, bias_ref.shape[0] - 1)]
        for r in range(dil):
            c = slice(r * LANES, (r + 1) * LANES)
            units.append((q_ref[i0:i0 + Q_BLOCK, c], k_ref[ks:ks + kw, c], v_ref[ks:ks + kw, c], bias))
            dests.append((blk, r))
    return units, dests


def _dilated_kernel(b1_ref, b4_ref, b16_ref, q1_ref, k1_ref, v1_ref, q4_ref, k4_ref, v4_ref,
                    q16_ref, k16_ref, v16_ref, out_ref, o_s, m_s, d_s, slab_s):
    lo = _lo_mask(Q_BLOCK)
    step = DILATIONS[1]
    assert DILATIONS[2] == step * step
    seq = out_ref.shape[0]

    units, dests = _band_units(b4_ref, q4_ref, k4_ref, v4_ref, step)
    for (blk, r), res in zip(dests, _pair_attention(units, lo, False)):
        rows = pl.ds(blk * (Q_BLOCK * step) + r, Q_BLOCK, stride=step)
        for ref, val in zip((o_s, m_s, d_s), res):
            ref[0, rows, :] = val

    units, dests = _band_units(b16_ref, q16_ref, k16_ref, v16_ref, DILATIONS[2])
    for (blk, r16), res in zip(dests, _pair_attention(units, lo, False)):
        r, u = r16 % step, r16 // step
        rows = pl.ds(blk * (Q_BLOCK * step) + u, Q_BLOCK, stride=step)
        for a, val in enumerate(res):
            slab_s[a * step + r, rows, :] = val
    for a, ref in enumerate((o_s, m_s, d_s)):
        for r in range(step):
            ref[1, pl.ds(r, seq // step, stride=step), :] = slab_s[a * step + r]

    units, dests = _band_units(b1_ref, q1_ref, k1_ref, v1_ref, DILATIONS[0])
    for (blk, _), (o1, m1, d1) in zip(dests, _pair_attention(units, lo, False)):
        rows = slice(blk * Q_BLOCK, (blk + 1) * Q_BLOCK)
        o2, m2, d2 = o_s[0, rows, :], m_s[0, rows, :], d_s[0, rows, :]
        o3, m3, d3 = o_s[1, rows, :], m_s[1, rows, :], d_s[1, rows, :]
        m = jnp.maximum(jnp.maximum(m1, m2), m3)
        e1, e2, e3 = jnp.exp2(m1 - m), jnp.exp2(m2 - m), jnp.exp2(m3 - m)
        out_ref[rows, :] = ((e1 * o1 + e2 * o2 + e3 * o3)
                            / (e1 * d1 + e2 * d2 + e3 * d3)).astype(BF16)


def _band_bias(length):
    kw = min(2 * Q_BLOCK, length)
    qi = np.arange(Q_BLOCK)[:, None]
    kj = np.arange(kw)[None, :]
    offsets = (0,) if length == Q_BLOCK else (0, -BAND_HALF, Q_BLOCK - kw)
    tiles = [np.where(np.abs(kj + off - qi) <= BAND_HALF, 0.0, NEG_INF) for off in offsets]
    return jnp.asarray(np.tile(np.stack(tiles), (1, 2, 1)), F32)


def _dilated_attention(qkv_a, batch, seq):
    pairs = WIDTH // LANES
    biases = [_band_bias(seq // dil) for dil in DILATIONS]
    spec = lambda dil: pl.BlockSpec((None, None, seq // dil, dil * LANES), lambda b, g: (b, g, 0, 0))
    return pl.pallas_call(
        _dilated_kernel,
        grid=(batch, pairs),
        in_specs=[_const_spec(b.shape) for b in biases]
                 + [spec(dil) for dil in DILATIONS for _ in range(3)],
        out_specs=pl.BlockSpec((None, seq, LANES), lambda b, g: (b, 0, g)),
        out_shape=jax.ShapeDtypeStruct((batch, seq, WIDTH), BF16),
        scratch_shapes=[pltpu.VMEM((len(DILATIONS) - 1, seq, LANES), F32)] * 3
                       + [pltpu.VMEM((3 * DILATIONS[1], seq // DILATIONS[1], LANES), F32)],
        compiler_params=_params("parallel", "parallel"),
        name="dilated",
    )(*biases, *[a for qkv in qkv_a for a in qkv])


def _nbr_kernel(bias_ref, q_ref, k_ref, v_ref, o_ref, *, rows):
    lo = _lo_mask(GRID_W)
    nkeys = NA_ROWS * GRID_W
    pairs = WIDTH // LANES
    rps = NBR_ROWS_PER_STEP

    def step(it, carry):
        units, dests = [], []
        for j in range(rps):
            r = it * rps + j
            rs = jnp.clip(r - NA_ROWS // 2, 0, rows - NA_ROWS)
            var = r - rs
            q0 = pl.multiple_of(r * GRID_W, GRID_W)
            k0 = pl.multiple_of(rs * GRID_W, GRID_W)
            for p in range(pairs):
                c = slice(p * LANES, (p + 1) * LANES)
                units.append((q_ref[pl.ds(q0, GRID_W), c], k_ref[pl.ds(k0, nkeys), c],
                              v_ref[pl.ds(k0, nkeys), c], bias_ref[p * NA_ROWS + var]))
                dests.append((q0, c))
        for (q0, c), (o, _, _) in zip(dests, _pair_attention(units, lo, True)):
            o_ref[pl.ds(q0, GRID_W), c] = o.astype(BF16)
        return carry

    lax.fori_loop(0, rows // rps, step, 0)


def _nbr_bias(rpb):
    c = np.arange(GRID_W)[:, None]
    ck = np.arange(GRID_W)[None, :]
    cs = np.clip(c - NA_COLS // 2, 0, GRID_W - NA_COLS)
    ok = (ck >= cs) & (ck < cs + NA_COLS)
    pad = GRID_W - NA_COLS
    rp = jnp.pad(rpb.astype(F32) * LOG2E, ((0, 0), (0, 0), (pad, pad)))
    u = jnp.pad(rp, ((0, 0), (0, 0), (0, 1))).reshape(N_HEADS, -1)
    period = u.shape[1]
    skew = jnp.tile(u, (1, GRID_W))[:, :GRID_W * (period - 1)].reshape(N_HEADS, GRID_W, period - 1)
    skew = jnp.pad(skew, ((0, 0), (0, 0), (0, 1))).reshape(N_HEADS, GRID_W, 2 * NA_ROWS - 1, 2 * GRID_W)
    t = skew[..., GRID_W - 1:2 * GRID_W - 1]
    t = jnp.where(ok[None, :, None, :], t, NEG_INF)
    t = t.reshape(N_HEADS // 2, 2 * GRID_W, 2 * NA_ROWS - 1, GRID_W)
    b = jnp.stack([t[:, :, NA_ROWS - 1 - v:2 * NA_ROWS - 1 - v] for v in range(NA_ROWS)], axis=1)
    return b.reshape(N_HEADS // 2 * NA_ROWS, 2 * GRID_W, NA_ROWS * GRID_W)


def _nbr_attention(q, k, v, bias, batch, seq):
    view = lambda a: a.reshape(batch, seq, WIDTH)
    spec = pl.BlockSpec((None, seq, WIDTH), lambda b: (b, 0, 0))
    o = pl.pallas_call(
        functools.partial(_nbr_kernel, rows=seq // GRID_W),
        grid=(batch,),
        in_specs=[_const_spec(bias.shape), spec, spec, spec],
        out_specs=spec,
        out_shape=jax.ShapeDtypeStruct((batch, seq, WIDTH), BF16),
        compiler_params=_params("parallel"),
        name="nbr",
    )(bias, view(q), view(k), view(v))
    return o.reshape(batch * seq, WIDTH)


def _merge_kernel(oa_ref, ob_ref, ga_ref, gb_ref,
                  x_ref, mod_ref, g_ref, wa_ref, wb_ref, wo_ref, out_ref):
    ya = jnp.dot(oa_ref[...], wa_ref[...], preferred_element_type=F32)
    yb = jnp.dot(ob_ref[...], wb_ref[...], preferred_element_type=F32)
    merged = ga_ref[...].astype(F32) * ya + gb_ref[...].astype(F32) * yb
    z = jnp.dot(merged.astype(BF16), wo_ref[...], preferred_element_type=F32)
    y = x_ref[...] + mod_ref[2:3, :] * _rms(z, g_ref[...])
    tiles, groups, rows, _ = out_ref.shape
    for n in range(tiles):
        for g in range(groups):
            out_ref[n, g] = y[n * rows:(n + 1) * rows, g * LANES:(g + 1) * LANES]


def _merge(oa, ob, ga, gb, x2, mod3, g_post, wa, wb, wo, seq, tm, out_tm):
    t, d = x2.shape
    tps = seq // tm
    assert tm % out_tm == 0
    row = lambda w: pl.BlockSpec((tm, w), lambda i: (i, 0))
    return pl.pallas_call(
        _merge_kernel,
        grid=(t // tm,),
        in_specs=[row(WIDTH)] * 2 + [row(d)] * 3
                 + [pl.BlockSpec((None, 6, d), lambda i: (i // tps, 0, 0)),
                    _const_spec((1, d)), _const_spec(wa.shape), _const_spec(wb.shape),
                    _const_spec(wo.shape)],
        out_specs=pl.BlockSpec((tm // out_tm, d // LANES, out_tm, LANES), lambda i: (i, 0, 0, 0)),
        out_shape=jax.ShapeDtypeStruct((t // out_tm, d // LANES, out_tm, LANES), F32),
        compiler_params=_params("parallel"),
        name="merge",
    )(oa, ob, ga, gb, x2, mod3, g_post, wa, wb, wo)


HALO = 16
FFN_TM = 512
FFN_SLOTS = 4
SLABS = 8
FFN_CHUNKS = (256, 768, 768, 768, 256)


def _ffn_kernel(x_ref, xp_ref, xn_ref, mod_ref, gpre_ref, gpost_ref, wup_ref, cw_ref, cb_ref,
                wdn_ref, out_ref, uv_ref, ug_ref, os_ref, *, tps, chunks):
    tm = x_ref.shape[1]
    slab = tm // SLABS
    lane_groups = range(D_MODEL // LANES)
    i = pl.program_id(0)
    has_prev = (i % tps != 0).astype(F32)
    has_next = (i % tps != tps - 1).astype(F32)
    shift, gain, gpre = mod_ref[3:4, :], 1.0 + mod_ref[4:5, :], gpre_ref[...]

    def prenorm(x):
        return _rms(x, gpre) * gain + shift

    x = jnp.concatenate(
        [jnp.concatenate([x_ref[g, pl.ds(j, slab, stride=SLABS), :] for g in lane_groups], axis=1)
         for j in range(SLABS)], axis=0)
    first = lax.broadcasted_iota(jnp.int32, (HALO, D_MODEL), 0) == 0
    token = lambda ref, r: jnp.concatenate([ref[g, r:r + 1, :] for g in lane_groups], axis=1)
    halo = jnp.where(first, token(xp_ref, HALO - 1), token(xn_ref, 0))
    hext = jnp.concatenate([prenorm(x).astype(BF16),
                            (prenorm(halo) * jnp.where(first, has_prev, has_next)).astype(BF16)], axis=0)
    prev_row, next_row = tm, tm + 1
    starts = [sum(chunks[:j]) for j in range(len(chunks))]

    def up(j):
        c0, w = starts[j], chunks[j]
        uv_ref[j % FFN_SLOTS, :, 0:w] = jnp.dot(hext, wup_ref[:, c0:c0 + w], preferred_element_type=F32)
        ug_ref[j % FFN_SLOTS, :, 0:w] = jnp.dot(hext, wup_ref[:, D_FF + c0:D_FF + c0 + w],
                                        preferred_element_type=F32)

    def activation(j):
        c0, w = starts[j], chunks[j]

        row = lax.broadcasted_iota(jnp.int32, (slab, w), 0)

        def conv(u_ref, col0, gain):
            cols = slice(col0, col0 + w)
            w0, w1, w2, b = (gain * t for t in (cw_ref[0:1, cols], cw_ref[1:2, cols],
                                                cw_ref[2:3, cols], cb_ref[:, cols]))
            rows = lambda r0: u_ref[j % FFN_SLOTS, pl.ds(r0, slab), 0:w]
            out = []
            for s in range(SLABS):
                prev = rows((s - 1) * slab) if s > 0 else jnp.where(
                    row == 0, u_ref[j % FFN_SLOTS, prev_row:prev_row + 1, 0:w], rows((SLABS - 1) * slab - 1))
                nxt = rows((s + 1) * slab) if s < SLABS - 1 else jnp.where(
                    row == slab - 1, u_ref[j % FFN_SLOTS, next_row:next_row + 1, 0:w], rows(1))
                out.append(w0 * prev + w1 * rows(s * slab) + w2 * nxt + b)
            return jnp.concatenate(out, axis=0)

        g = conv(ug_ref, D_FF + c0, 1.0)
        gv = g * conv(uv_ref, c0, 0.5)
        t = jnp.tanh(g * (GELU_C + (GELU_C * GELU_CUBIC) * (g * g)))
        return (gv + gv * t).astype(BF16)

    ahead = FFN_SLOTS - 1
    for j in range(min(ahead, len(chunks))):
        up(j)
    acc = jnp.zeros((tm, D_MODEL), F32)
    for j in range(len(chunks)):
        if j + ahead < len(chunks):
            up(j + ahead)
        c0, w = starts[j], chunks[j]
        acc += jnp.dot(activation(j), wdn_ref[c0:c0 + w, :], preferred_element_type=F32)
    y = x + mod_ref[5:6, :] * _rms(acc, gpost_ref[...])
    for s in range(SLABS):
        for g in lane_groups:
            os_ref[g, pl.ds(s, slab, stride=SLABS), :] = y[s * slab:(s + 1) * slab,
                                                           g * LANES:(g + 1) * LANES]
    for g in lane_groups:
        out_ref[:, g * LANES:(g + 1) * LANES] = os_ref[g]


def _ffn(x1, mod3, g_pre, g_post, w_up, conv_w, conv_b, w_down, seq, tm, chunks):
    assert sum(chunks) == D_FF and all(c % LANES == 0 for c in chunks)
    tiles, groups, tile_rows, _ = x1.shape
    assert tile_rows == tm
    t, d = tiles * tm, groups * LANES
    tps = seq // tm
    halo = lambda tile, blk: pl.BlockSpec((None, groups, HALO, LANES), lambda i: (tile(i), 0, blk, 0))
    return pl.pallas_call(
        functools.partial(_ffn_kernel, tps=tps, chunks=chunks),
        grid=(tiles,),
        in_specs=[pl.BlockSpec((None, groups, tm, LANES), lambda i: (i, 0, 0, 0)),
                  halo(lambda i: jnp.maximum(i - 1, 0), tm // HALO - 1),
                  halo(lambda i: jnp.minimum(i + 1, tiles - 1), 0),
                  pl.BlockSpec((None, 6, d), lambda i: (i // tps, 0, 0)),
                  _const_spec((1, d)), _const_spec((1, d)),
                  _const_spec(w_up.shape), _const_spec(conv_w.shape), _const_spec(conv_b.shape),
                  _const_spec(w_down.shape)],
        out_specs=pl.BlockSpec((tm, d), lambda i: (i, 0)),
        out_shape=jax.ShapeDtypeStruct((t, d), F32),
        scratch_shapes=[pltpu.VMEM((FFN_SLOTS, tm + HALO, max(chunks)), F32)] * 2
                       + [pltpu.VMEM((groups, tm, LANES), F32)],
        compiler_params=_params("parallel"),
        name="ffn",
    )(x1, x1, x1, mod3, g_pre, g_post, w_up, conv_w, conv_b, w_down)


def _layer(x, mod3, p, tables, nbr_bias):
    batch, seq, d = x.shape
    x2 = x.reshape(batch * seq, d)
    qkv_a, qkv_b, ga, gb = _inproj(x2, mod3, p["g_mix_pre"], p["w_in"], tables, seq, INPROJ_TM)
    oa = _dilated_attention(qkv_a, batch, seq).reshape(batch * seq, WIDTH)
    ob = _nbr_attention(*qkv_b, nbr_bias, batch, seq)
    x1 = _merge(oa, ob, ga, gb, x2, mod3,
                p["g_mix_post"], p["w_branch_a"], p["w_branch_b"], p["w_out"], seq, MERGE_TM, FFN_TM)
    y = _ffn(x1, mod3, p["g_ffn_pre"], p["g_ffn_post"], p["w_up"], p["conv_w"], p["conv_b"],
             p["w_down"], seq, FFN_TM, FFN_CHUNKS)
    return y.reshape(batch, seq, d)


def kernel(x_prompt, x_sample, c_prompt, c_sample, w_ada, b_ada, g_mix_pre, g_mix_post, g_ffn_pre,
           g_ffn_post, w_in, rpb, w_branch_a, w_branch_b, w_out, w_up, conv_w, conv_b, w_down):
    assert w_ada.shape[0] == 1, "single-layer trunk"
    row = lambda a: a[0].reshape(1, -1)
    p = {
        "w_ada": w_ada[0], "b_ada": b_ada[0],
        "g_mix_pre": row(g_mix_pre), "g_mix_post": row(g_mix_post),
        "g_ffn_pre": row(g_ffn_pre), "g_ffn_post": row(g_ffn_post),
        "w_in": (w_in[0][:, :6 * WIDTH].astype(BF16), w_in[0][:, 6 * WIDTH:].astype(BF16)),
        "w_branch_a": w_branch_a[0].astype(BF16), "w_branch_b": w_branch_b[0].astype(BF16),
        "w_out": w_out[0].astype(BF16), "w_up": w_up[0].astype(BF16),
        "conv_w": conv_w[0], "conv_b": row(conv_b), "w_down": w_down[0].astype(BF16),
    }
    nbr_bias = _nbr_bias(rpb[0])
    tables = _rope_tables(x_prompt.shape[1])
    n_prompt = x_prompt.shape[0]
    mod = _modulation(jnp.concatenate([c_prompt, c_sample], axis=0), p["w_ada"], p["b_ada"])
    mod3 = mod.reshape(mod.shape[0], 6, D_MODEL)
    y_prompt = _layer(x_prompt, mod3[:n_prompt], p, tables, nbr_bias)
    y_sample = _layer(x_sample, mod3[n_prompt:], p, tables, nbr_bias)
    return (y_prompt, y_sample)
```

```python
import functools

import numpy as np
import jax
import jax.numpy as jnp
from jax import lax
from jax.experimental import pallas as pl
from jax.experimental.pallas import tpu as pltpu

D_MODEL = 1024
HEAD_DIM = 64
N_HEADS = 8
WIDTH = N_HEADS * HEAD_DIM
DILATIONS = (1, 4, 16)
BAND_HALF = 64
ROPE_THETA = 500000.0
ROPE_DIM = HEAD_DIM // 4
GRID_W = 64
NA_ROWS = 8
NA_COLS = 16
D_FF = 2816
EPS = 1e-6
NEG_INF = -1e30
LOG2E = 1.4426950408889634
GELU_C = 0.7978845608028654
GELU_CUBIC = 0.044715

LANES = 128
Q_BLOCK = 128
NBR_ROWS_PER_STEP = 16
VMEM_LIMIT = 56 * 1024 * 1024
INPROJ_TM = 1024
MERGE_TM = 1024

F32 = jnp.float32
BF16 = jnp.bfloat16


def _params(*sem):
    return pltpu.CompilerParams(dimension_semantics=sem, vmem_limit_bytes=VMEM_LIMIT)


def _const_spec(shape):
    nd = len(shape)
    return pl.BlockSpec(shape, lambda *_: (0,) * nd, pipeline_mode=pl.Buffered(1))


def _rms(x, g):
    return x * lax.rsqrt(jnp.mean(x * x, axis=-1, keepdims=True) + EPS) * g


def _mod_kernel(c_ref, w_ref, b_ref, o_ref):
    c = c_ref[...]
    s = c * jax.nn.sigmoid(c)
    w = w_ref[...]
    s_hi = s.astype(BF16)
    s_lo = (s - s_hi.astype(F32)).astype(BF16)
    w_hi = w.astype(BF16)
    w_lo = (w - w_hi.astype(F32)).astype(BF16)
    acc = jnp.dot(s_hi, w_hi, preferred_element_type=F32)
    acc += jnp.dot(s_hi, w_lo, preferred_element_type=F32)
    acc += jnp.dot(s_lo, w_hi, preferred_element_type=F32)
    o_ref[...] = acc + b_ref[...]


def _modulation(c, w_ada, b_ada):
    b, d = c.shape
    n = w_ada.shape[1]
    return pl.pallas_call(
        _mod_kernel,
        grid=(n // d,),
        in_specs=[_const_spec((b, d)),
                  pl.BlockSpec((d, d), lambda j: (0, j)),
                  pl.BlockSpec((1, d), lambda j: (0, j))],
        out_specs=pl.BlockSpec((b, d), lambda j: (0, j)),
        out_shape=jax.ShapeDtypeStruct((b, n), F32),
        compiler_params=_params("arbitrary"),
        name="mod",
    )(c, w_ada, b_ada.reshape(1, n))


def _gates_kernel(x_ref, mod_ref, g_ref, w_ref, ga_ref, gb_ref):
    h = _rms(x_ref[...], g_ref[...]) * (1.0 + mod_ref[1:2, :]) + mod_ref[0:1, :]
    hb = h.astype(BF16)
    for c0, ref in ((0, ga_ref), (D_MODEL, gb_ref)):
        ref[...] = jax.nn.sigmoid(jnp.dot(hb, w_ref[:, c0:c0 + D_MODEL],
                                          preferred_element_type=F32)).astype(BF16)


def _inproj_kernel(x_ref, mod_ref, g_ref, w_ref, cos_ref, sa_ref, sb_ref,
                   q1_ref, k1_ref, v1_ref, q4_ref, k4_ref, v4_ref, q16_ref, k16_ref, v16_ref,
                   qb_ref, kb_ref, vb_ref, stage_ref, slab_ref):
    tm = x_ref.shape[0]
    h = _rms(x_ref[...], g_ref[...]) * (1.0 + mod_ref[1:2, :]) + mod_ref[0:1, :]
    hb = h.astype(BF16)

    def proj(c0, width):
        return jnp.dot(hb, w_ref[:, c0:c0 + width], preferred_element_type=F32)

    cos, sa, sb = cos_ref[...], sa_ref[...], sb_ref[...]

    def rope(t):
        outs = []
        for g in range(t.shape[1] // LANES):
            tg = t[:, g * LANES:(g + 1) * LANES]
            outs.append(tg * cos + pltpu.roll(tg, ROPE_DIM // 2, 1) * sa
                        + pltpu.roll(tg, LANES - ROPE_DIM // 2, 1) * sb)
        return jnp.concatenate(outs, axis=1)

    def emit(t, refs):
        ref1, ref4, ref16 = refs
        step = DILATIONS[1]
        assert DILATIONS[2] == step * step
        for g in range(WIDTH // LANES):
            tg = t[:, g * LANES:(g + 1) * LANES]
            ref1[g] = tg.astype(BF16)
            stage_ref[g] = tg
            for r in range(step):
                slab = stage_ref[g, pl.ds(r, tm // step, stride=step), :]
                ref4[g, :, r * LANES:(r + 1) * LANES] = slab.astype(BF16)
                slab_ref[g * step + r] = slab
        for r16 in range(DILATIONS[2]):
            r, u = r16 % step, r16 // step
            for g in range(WIDTH // LANES):
                ref16[g, :, r16 * LANES:(r16 + 1) * LANES] = (
                    slab_ref[g * step + r, pl.ds(u, tm // DILATIONS[2], stride=step), :].astype(BF16))

    scale = HEAD_DIM ** -0.5 * LOG2E

    emit(rope(proj(0, WIDTH)) * scale, (q1_ref, q4_ref, q16_ref))
    emit(rope(proj(WIDTH, WIDTH)), (k1_ref, k4_ref, k16_ref))
    emit(proj(2 * WIDTH, WIDTH), (v1_ref, v4_ref, v16_ref))
    qb_ref[...] = (proj(3 * WIDTH, WIDTH) * scale).astype(BF16)
    kb_ref[...] = proj(4 * WIDTH, WIDTH).astype(BF16)
    vb_ref[...] = proj(5 * WIDTH, WIDTH).astype(BF16)


def _rope_tables(seq):
    half = ROPE_DIM // 2
    inv = jnp.power(jnp.float32(ROPE_THETA), -jnp.arange(half, dtype=F32) / half)
    ang = jnp.arange(seq).astype(F32)[:, None] * inv[None, :]
    cos, sin = jnp.cos(ang), jnp.sin(ang)
    ones = jnp.ones((seq, HEAD_DIM - ROPE_DIM), F32)
    zeros = jnp.zeros((seq, HEAD_DIM - ROPE_DIM), F32)
    zh = jnp.zeros((seq, half), F32)
    per_head = lambda *parts: jnp.tile(jnp.concatenate(parts, axis=1), (1, LANES // HEAD_DIM))
    return (per_head(cos, cos, ones),
            per_head(zh, sin, zeros),
            per_head(-sin, zh, zeros))


def _inproj(x2, mod3, g_pre, w_in, tables, seq, tm):
    t, d = x2.shape
    tps = seq // tm
    row = lambda w: pl.BlockSpec((tm, w), lambda i: (i, 0))
    tab = pl.BlockSpec((tm, LANES), lambda i: (i % tps, 0))
    pairs = WIDTH // LANES
    dil_shapes = [jax.ShapeDtypeStruct((t // seq, pairs, seq // dil, dil * LANES), BF16)
                  for dil in DILATIONS]
    dil_specs = [pl.BlockSpec((None, pairs, tm // dil, dil * LANES),
                              lambda i: (i // tps, 0, i % tps, 0)) for dil in DILATIONS]
    w_qkv, w_gates = w_in
    common = [row(d), pl.BlockSpec((None, 6, d), lambda i: (i // tps, 0, 0)), _const_spec((1, d))]
    outs = pl.pallas_call(
        _inproj_kernel,
        grid=(t // tm,),
        in_specs=common + [_const_spec(w_qkv.shape), tab, tab, tab],
        out_specs=[s for s in dil_specs for _ in range(3)] + [row(WIDTH)] * 3,
        out_shape=[s for s in dil_shapes for _ in range(3)]
                  + [jax.ShapeDtypeStruct((t, WIDTH), BF16)] * 3,
        scratch_shapes=[pltpu.VMEM((pairs, tm, LANES), F32),
                        pltpu.VMEM((pairs * DILATIONS[1], tm // DILATIONS[1], LANES), F32)],
        compiler_params=_params("parallel"),
        name="inproj",
    )(x2, mod3, g_pre, w_qkv, *tables)
    ga, gb = pl.pallas_call(
        _gates_kernel,
        grid=(t // tm,),
        in_specs=common + [_const_spec(w_gates.shape)],
        out_specs=[row(d)] * 2,
        out_shape=[jax.ShapeDtypeStruct((t, d), BF16)] * 2,
        compiler_params=_params("parallel"),
        name="gates",
    )(x2, mod3, g_pre, w_gates)
    qkv_a = [outs[3 * n:3 * n + 3] for n in range(len(DILATIONS))]
    return qkv_a, outs[9:12], ga, gb


def _pair_attention(units, lo_mask, normalise):
    zero = jnp.zeros_like(units[0][0])
    scores = []
    for q2, kwin, _, bias in units:
        qx = jnp.concatenate([jnp.where(lo_mask, q2, zero), jnp.where(lo_mask, zero, q2)], axis=0)
        s = lax.dot_general(qx, kwin, (((1,), (1,)), ((), ())), preferred_element_type=F32)
        scores.append(s + bias)
    maxes = [jnp.max(s, axis=1, keepdims=True) for s in scores]
    probs = [jnp.exp2(s - m).astype(BF16) for s, m in zip(scores, maxes)]
    for p, m, (q2, _, vwin, _) in zip(probs, maxes, units):
        rows = q2.shape[0]
        if isinstance(vwin, tuple):
            v_lo, v_hi = vwin
            key_lo = lax.broadcasted_iota(jnp.int32, (v_lo.shape[0], LANES), 1) < HEAD_DIM
            ones_lo = jnp.where(key_lo, 1.0, 0.0).astype(BF16)
            ones_hi = jnp.where(key_lo, 0.0, 1.0).astype(BF16)
            rhs = jnp.concatenate([jnp.concatenate([v_lo, ones_lo], axis=1),
                                   jnp.concatenate([v_hi, ones_hi], axis=1)], axis=0)
            r = jnp.dot(jnp.concatenate([p[:rows], p[rows:]], axis=1), rhs,
                        preferred_element_type=F32)
            o, den = r[:, :LANES], r[:, LANES:]
            if normalise:
                o = o / den
            yield o, jnp.where(lo_mask, jnp.broadcast_to(m[:rows], o.shape),
                               jnp.broadcast_to(m[rows:], o.shape)), den
            continue
        vaug = jnp.concatenate([vwin, jnp.ones((vwin.shape[0], LANES), BF16)], axis=1)
        r = jnp.dot(p, vaug, preferred_element_type=F32)
        o, den = r[:, :LANES], r[:, LANES:]
        if normalise:
            o = o / den
        pick = lambda t: jnp.where(lo_mask, t[:rows], t[rows:])
        yield pick(o), pick(jnp.broadcast_to(m, den.shape)), pick(den)


def _lo_mask(rows):
    return lax.broadcasted_iota(jnp.int32, (rows, LANES), 1) < HEAD_DIM


def _split_heads(v_ref, split_ref):
    n = min(2 * Q_BLOCK, v_ref.shape[0])
    assert v_ref.shape[0] % n == 0
    lo = _lo_mask(n)
    zero = jnp.zeros((n, LANES), v_ref.dtype)
    for r0 in range(0, v_ref.shape[0], n):
        for c0 in range(0, v_ref.shape[1], LANES):
            v = v_ref[r0:r0 + n, c0:c0 + LANES]
            split_ref[0, r0:r0 + n, c0:c0 + LANES] = jnp.where(lo, v, zero)
            split_ref[1, r0:r0 + n, c0:c0 + LANES] = jnp.where(lo, zero, v)


def _band_units(bias_ref, q_ref, k_ref, v_ref, dil):
    length = q_ref.shape[0]
    nblk = length // Q_BLOCK
    kw = bias_ref.shape[2]
    units, dests = [], []
    for blk in range(nblk):
        i0 = blk * Q_BLOCK
        ks = min(max(i0 - BAND_HALF, 0), length - kw)
        kind = 0 if blk == 0 else (2 if blk == nblk - 1 else 1)
        bias = bias_ref[min(kind, bias_ref.shape[0] - 1)]
        for r in range(dil):
            c = slice(r * LANES, (r + 1) * LANES)
            units.append((q_ref[i0:i0 + Q_BLOCK, c], k_ref[ks:ks + kw, c],
                          (v_ref[0, ks:ks + kw, c], v_ref[1, ks:ks + kw, c]), bias))
            dests.append((blk, r))
    return units, dests


def _dilated_kernel(b1_ref, b4_ref, b16_ref, q1_ref, k1_ref, v1_ref, q4_ref, k4_ref, v4_ref,
                    q16_ref, k16_ref, v16_ref, out_ref, o_s, m_s, d_s, slab_s, vs1_ref, vs4_ref, vs16_ref):
    lo = _lo_mask(Q_BLOCK)
    step = DILATIONS[1]
    assert DILATIONS[2] == step * step
    seq = out_ref.shape[0]
    for v_ref, split_ref in ((v1_ref, vs1_ref), (v4_ref, vs4_ref), (v16_ref, vs16_ref)):
        _split_heads(v_ref, split_ref)

    units, dests = _band_units(b4_ref, q4_ref, k4_ref, vs4_ref, step)
    for (blk, r), res in zip(dests, _pair_attention(units, lo, False)):
        rows = pl.ds(blk * (Q_BLOCK * step) + r, Q_BLOCK, stride=step)
        for ref, val in zip((o_s, m_s, d_s), res):
            ref[0, rows, :] = val

    units, dests = _band_units(b16_ref, q16_ref, k16_ref, vs16_ref, DILATIONS[2])
    for (blk, r16), res in zip(dests, _pair_attention(units, lo, False)):
        r, u = r16 % step, r16 // step
        rows = pl.ds(blk * (Q_BLOCK * step) + u, Q_BLOCK, stride=step)
        for a, val in enumerate(res):
            slab_s[a * step + r, rows, :] = val
    for a, ref in enumerate((o_s, m_s, d_s)):
        for r in range(step):
            ref[1, pl.ds(r, seq // step, stride=step), :] = slab_s[a * step + r]

    units, dests = _band_units(b1_ref, q1_ref, k1_ref, vs1_ref, DILATIONS[0])
    for (blk, _), (o1, m1, d1) in zip(dests, _pair_attention(units, lo, False)):
        rows = slice(blk * Q_BLOCK, (blk + 1) * Q_BLOCK)
        o2, m2, d2 = o_s[0, rows, :], m_s[0, rows, :], d_s[0, rows, :]
        o3, m3, d3 = o_s[1, rows, :], m_s[1, rows, :], d_s[1, rows, :]
        m = jnp.maximum(jnp.maximum(m1, m2), m3)
        e1, e2, e3 = jnp.exp2(m1 - m), jnp.exp2(m2 - m), jnp.exp2(m3 - m)
        out_ref[rows, :] = ((e1 * o1 + e2 * o2 + e3 * o3)
                            / (e1 * d1 + e2 * d2 + e3 * d3)).astype(BF16)


def _band_bias(length):
    kw = min(2 * Q_BLOCK, length)
    qi = np.arange(Q_BLOCK)[:, None]
    kj = np.arange(kw)[None, :]
    offsets = (0,) if length == Q_BLOCK else (0, -BAND_HALF, Q_BLOCK - kw)
    tiles = [np.where(np.abs(kj + off - qi) <= BAND_HALF, 0.0, NEG_INF) for off in offsets]
    return jnp.asarray(np.tile(np.stack(tiles), (1, 2, 1)), F32)


def _dilated_attention(qkv_a, batch, seq):
    pairs = WIDTH // LANES
    biases = [_band_bias(seq // dil) for dil in DILATIONS]
    spec = lambda dil: pl.BlockSpec((None, None, seq // dil, dil * LANES), lambda b, g: (b, g, 0, 0))
    return pl.pallas_call(
        _dilated_kernel,
        grid=(batch, pairs),
        in_specs=[_const_spec(b.shape) for b in biases]
                 + [spec(dil) for dil in DILATIONS for _ in range(3)],
        out_specs=pl.BlockSpec((None, seq, LANES), lambda b, g: (b, 0, g)),
        out_shape=jax.ShapeDtypeStruct((batch, seq, WIDTH), BF16),
        scratch_shapes=[pltpu.VMEM((len(DILATIONS) - 1, seq, LANES), F32)] * 3
                       + [pltpu.VMEM((3 * DILATIONS[1], seq // DILATIONS[1], LANES), F32)]
                       + [pltpu.VMEM((2, seq // dil, dil * LANES), BF16) for dil in DILATIONS],
        compiler_params=_params("parallel", "parallel"),
        name="dilated",
    )(*biases, *[a for qkv in qkv_a for a in qkv])


def _nbr_kernel(bias_ref, q_ref, k_ref, v_ref, o_ref, *, rows):
    lo = _lo_mask(GRID_W)
    nkeys = NA_ROWS * GRID_W
    pairs = WIDTH // LANES
    rps = NBR_ROWS_PER_STEP

    def step(it, carry):
        units, dests = [], []
        for j in range(rps):
            r = it * rps + j
            rs = jnp.clip(r - NA_ROWS // 2, 0, rows - NA_ROWS)
            var = r - rs
            q0 = pl.multiple_of(r * GRID_W, GRID_W)
            k0 = pl.multiple_of(rs * GRID_W, GRID_W)
            for p in range(pairs):
                c = slice(p * LANES, (p + 1) * LANES)
                units.append((q_ref[pl.ds(q0, GRID_W), c], k_ref[pl.ds(k0, nkeys), c],
                              v_ref[pl.ds(k0, nkeys), c], bias_ref[p * NA_ROWS + var]))
                dests.append((q0, c))
        for (q0, c), (o, _, _) in zip(dests, _pair_attention(units, lo, True)):
            o_ref[pl.ds(q0, GRID_W), c] = o.astype(BF16)
        return carry

    lax.fori_loop(0, rows // rps, step, 0)


def _nbr_bias(rpb):
    c = np.arange(GRID_W)[:, None]
    ck = np.arange(GRID_W)[None, :]
    cs = np.clip(c - NA_COLS // 2, 0, GRID_W - NA_COLS)
    ok = (ck >= cs) & (ck < cs + NA_COLS)
    pad = GRID_W - NA_COLS
    rp = jnp.pad(rpb.astype(F32) * LOG2E, ((0, 0), (0, 0), (pad, pad)))
    u = jnp.pad(rp, ((0, 0), (0, 0), (0, 1))).reshape(N_HEADS, -1)
    period = u.shape[1]
    skew = jnp.tile(u, (1, GRID_W))[:, :GRID_W * (period - 1)].reshape(N_HEADS, GRID_W, period - 1)
    skew = jnp.pad(skew, ((0, 0), (0, 0), (0, 1))).reshape(N_HEADS, GRID_W, 2 * NA_ROWS - 1, 2 * GRID_W)
    t = skew[..., GRID_W - 1:2 * GRID_W - 1]
    t = jnp.where(ok[None, :, None, :], t, NEG_INF)
    t = t.reshape(N_HEADS // 2, 2 * GRID_W, 2 * NA_ROWS - 1, GRID_W)
    b = jnp.stack([t[:, :, NA_ROWS - 1 - v:2 * NA_ROWS - 1 - v] for v in range(NA_ROWS)], axis=1)
    return b.reshape(N_HEADS // 2 * NA_ROWS, 2 * GRID_W, NA_ROWS * GRID_W)


def _nbr_attention(q, k, v, bias, batch, seq):
    view = lambda a: a.reshape(batch, seq, WIDTH)
    spec = pl.BlockSpec((None, seq, WIDTH), lambda b: (b, 0, 0))
    o = pl.pallas_call(
        functools.partial(_nbr_kernel, rows=seq // GRID_W),
        grid=(batch,),
        in_specs=[_const_spec(bias.shape), spec, spec, spec],
        out_specs=spec,
        out_shape=jax.ShapeDtypeStruct((batch, seq, WIDTH), BF16),
        compiler_params=_params("parallel"),
        name="nbr",
    )(bias, view(q), view(k), view(v))
    return o.reshape(batch * seq, WIDTH)


def _merge_kernel(oa_ref, ob_ref, ga_ref, gb_ref,
                  x_ref, mod_ref, g_ref, wa_ref, wb_ref, wo_ref, out_ref):
    ya = jnp.dot(oa_ref[...], wa_ref[...], preferred_element_type=F32)
    yb = jnp.dot(ob_ref[...], wb_ref[...], preferred_element_type=F32)
    merged = ga_ref[...].astype(F32) * ya + gb_ref[...].astype(F32) * yb
    z = jnp.dot(merged.astype(BF16), wo_ref[...], preferred_element_type=F32)
    y = x_ref[...] + mod_ref[2:3, :] * _rms(z, g_ref[...])
    tiles, groups, rows, _ = out_ref.shape
    for n in range(tiles):
        for g in range(groups):
            out_ref[n, g] = y[n * rows:(n + 1) * rows, g * LANES:(g + 1) * LANES]


def _merge(oa, ob, ga, gb, x2, mod3, g_post, wa, wb, wo, seq, tm, out_tm):
    t, d = x2.shape
    tps = seq // tm
    assert tm % out_tm == 0
    row = lambda w: pl.BlockSpec((tm, w), lambda i: (i, 0))
    return pl.pallas_call(
        _merge_kernel,
        grid=(t // tm,),
        in_specs=[row(WIDTH)] * 2 + [row(d)] * 3
                 + [pl.BlockSpec((None, 6, d), lambda i: (i // tps, 0, 0)),
                    _const_spec((1, d)), _const_spec(wa.shape), _const_spec(wb.shape),
                    _const_spec(wo.shape)],
        out_specs=pl.BlockSpec((tm // out_tm, d // LANES, out_tm, LANES), lambda i: (i, 0, 0, 0)),
        out_shape=jax.ShapeDtypeStruct((t // out_tm, d // LANES, out_tm, LANES), F32),
        compiler_params=_params("parallel"),
        name="merge",
    )(oa, ob, ga, gb, x2, mod3, g_post, wa, wb, wo)


HALO = 16
FFN_TM = 512
FFN_SLOTS = 4
SLABS = 8
FFN_CHUNKS = (256, 768, 768, 768, 256)


def _ffn_kernel(x_ref, xp_ref, xn_ref, mod_ref, gpre_ref, gpost_ref, wup_ref, cw_ref, cb_ref,
                wdn_ref, out_ref, uv_ref, ug_ref, os_ref, *, tps, chunks):
    tm = x_ref.shape[1]
    slab = tm // SLABS
    lane_groups = range(D_MODEL // LANES)
    i = pl.program_id(0)
    has_prev = (i % tps != 0).astype(F32)
    has_next = (i % tps != tps - 1).astype(F32)
    shift, gain, gpre = mod_ref[3:4, :], 1.0 + mod_ref[4:5, :], gpre_ref[...]

    def prenorm(x):
        return _rms(x, gpre) * gain + shift

    x = jnp.concatenate(
        [jnp.concatenate([x_ref[g, pl.ds(j, slab, stride=SLABS), :] for g in lane_groups], axis=1)
         for j in range(SLABS)], axis=0)
    first = lax.broadcasted_iota(jnp.int32, (HALO, D_MODEL), 0) == 0
    token = lambda ref, r: jnp.concatenate([ref[g, r:r + 1, :] for g in lane_groups], axis=1)
    halo = jnp.where(first, token(xp_ref, HALO - 1), token(xn_ref, 0))
    hext = jnp.concatenate([prenorm(x).astype(BF16),
                            (prenorm(halo) * jnp.where(first, has_prev, has_next)).astype(BF16)], axis=0)
    prev_row, next_row = tm, tm + 1
    starts = [sum(chunks[:j]) for j in range(len(chunks))]

    def up(j):
        c0, w = starts[j], chunks[j]
        uv_ref[j % FFN_SLOTS, :, 0:w] = jnp.dot(hext, wup_ref[:, c0:c0 + w], preferred_element_type=F32)
        ug_ref[j % FFN_SLOTS, :, 0:w] = jnp.dot(hext, wup_ref[:, D_FF + c0:D_FF + c0 + w],
                                        preferred_element_type=F32)

    def activation(j):
        c0, w = starts[j], chunks[j]

        row = lax.broadcasted_iota(jnp.int32, (slab, w), 0)

        def conv(u_ref, col0, gain):
            cols = slice(col0, col0 + w)
            w0, w1, w2, b = (gain * t for t in (cw_ref[0:1, cols], cw_ref[1:2, cols],
                                                cw_ref[2:3, cols], cb_ref[:, cols]))
            rows = lambda r0: u_ref[j % FFN_SLOTS, pl.ds(r0, slab), 0:w]
            out = []
            for s in range(SLABS):
                prev = rows((s - 1) * slab) if s > 0 else jnp.where(
                    row == 0, u_ref[j % FFN_SLOTS, prev_row:prev_row + 1, 0:w], rows((SLABS - 1) * slab - 1))
                nxt = rows((s + 1) * slab) if s < SLABS - 1 else jnp.where(
                    row == slab - 1, u_ref[j % FFN_SLOTS, next_row:next_row + 1, 0:w], rows(1))
                out.append(w0 * prev + w1 * rows(s * slab) + w2 * nxt + b)
            return jnp.concatenate(out, axis=0)

        g = conv(ug_ref, D_FF + c0, 1.0)
        gv = g * conv(uv_ref, c0, 0.5)
        t = jnp.tanh(g * (GELU_C + (GELU_C * GELU_CUBIC) * (g * g)))
        return (gv + gv * t).astype(BF16)

    ahead = FFN_SLOTS - 1
    for j in range(min(ahead, len(chunks))):
        up(j)
    acc = jnp.zeros((tm, D_MODEL), F32)
    for j in range(len(chunks)):
        if j + ahead < len(chunks):
            up(j + ahead)
        c0, w = starts[j], chunks[j]
        acc += jnp.dot(activation(j), wdn_ref[c0:c0 + w, :], preferred_element_type=F32)
    y = x + mod_ref[5:6, :] * _rms(acc, gpost_ref[...])
    for s in range(SLABS):
        for g in lane_groups:
            os_ref[g, pl.ds(s, slab, stride=SLABS), :] = y[s * slab:(s + 1) * slab,
                                                           g * LANES:(g + 1) * LANES]
    for g in lane_groups:
        out_ref[:, g * LANES:(g + 1) * LANES] = os_ref[g]


def _ffn(x1, mod3, g_pre, g_post, w_up, conv_w, conv_b, w_down, seq, tm, chunks):
    assert sum(chunks) == D_FF and all(c % LANES == 0 for c in chunks)
    tiles, groups, tile_rows, _ = x1.shape
    assert tile_rows == tm
    t, d = tiles * tm, groups * LANES
    tps = seq // tm
    halo = lambda tile, blk: pl.BlockSpec((None, groups, HALO, LANES), lambda i: (tile(i), 0, blk, 0))
    return pl.pallas_call(
        functools.partial(_ffn_kernel, tps=tps, chunks=chunks),
        grid=(tiles,),
        in_specs=[pl.BlockSpec((None, groups, tm, LANES), lambda i: (i, 0, 0, 0)),
                  halo(lambda i: jnp.maximum(i - 1, 0), tm // HALO - 1),
                  halo(lambda i: jnp.minimum(i + 1, tiles - 1), 0),
                  pl.BlockSpec((None, 6, d), lambda i: (i // tps, 0, 0)),
                  _const_spec((1, d)), _const_spec((1, d)),
                  _const_spec(w_up.shape), _const_spec(conv_w.shape), _const_spec(conv_b.shape),
                  _const_spec(w_down.shape)],
        out_specs=pl.BlockSpec((tm, d), lambda i: (i, 0)),
        out_shape=jax.ShapeDtypeStruct((t, d), F32),
        scratch_shapes=[pltpu.VMEM((FFN_SLOTS, tm + HALO, max(chunks)), F32)] * 2
                       + [pltpu.VMEM((groups, tm, LANES), F32)],
        compiler_params=_params("parallel"),
        name="ffn",
    )(x1, x1, x1, mod3, g_pre, g_post, w_up, conv_w, conv_b, w_down)


def _layer(x, mod3, p, tables, nbr_bias):
    batch, seq, d = x.shape
    x2 = x.reshape(batch * seq, d)
    qkv_a, qkv_b, ga, gb = _inproj(x2, mod3, p["g_mix_pre"], p["w_in"], tables, seq, INPROJ_TM)
    oa = _dilated_attention(qkv_a, batch, seq).reshape(batch * seq, WIDTH)
    ob = _nbr_attention(*qkv_b, nbr_bias, batch, seq)
    x1 = _merge(oa, ob, ga, gb, x2, mod3,
                p["g_mix_post"], p["w_branch_a"], p["w_branch_b"], p["w_out"], seq, MERGE_TM, FFN_TM)
    y = _ffn(x1, mod3, p["g_ffn_pre"], p["g_ffn_post"], p["w_up"], p["conv_w"], p["conv_b"],
             p["w_down"], seq, FFN_TM, FFN_CHUNKS)
    return y.reshape(batch, seq, d)


def kernel(x_prompt, x_sample, c_prompt, c_sample, w_ada, b_ada, g_mix_pre, g_mix_post, g_ffn_pre,
           g_ffn_post, w_in, rpb, w_branch_a, w_branch_b, w_out, w_up, conv_w, conv_b, w_down):
    assert w_ada.shape[0] == 1, "single-layer trunk"
    row = lambda a: a[0].reshape(1, -1)
    p = {
        "w_ada": w_ada[0], "b_ada": b_ada[0],
        "g_mix_pre": row(g_mix_pre), "g_mix_post": row(g_mix_post),
        "g_ffn_pre": row(g_ffn_pre), "g_ffn_post": row(g_ffn_post),
        "w_in": (w_in[0][:, :6 * WIDTH].astype(BF16), w_in[0][:, 6 * WIDTH:].astype(BF16)),
        "w_branch_a": w_branch_a[0].astype(BF16), "w_branch_b": w_branch_b[0].astype(BF16),
        "w_out": w_out[0].astype(BF16), "w_up": w_up[0].astype(BF16),
        "conv_w": conv_w[0], "conv_b": row(conv_b), "w_down": w_down[0].astype(BF16),
    }
    nbr_bias = _nbr_bias(rpb[0])
    tables = _rope_tables(x_prompt.shape[1])
    n_prompt = x_prompt.shape[0]
    mod = _modulation(jnp.concatenate([c_prompt, c_sample], axis=0), p["w_ada"], p["b_ada"])
    mod3 = mod.reshape(mod.shape[0], 6, D_MODEL)
    y_prompt = _layer(x_prompt, mod3[:n_prompt], p, tables, nbr_bias)
    y_sample = _layer(x_sample, mod3[n_prompt:], p, tables, nbr_bias)
    return (y_prompt, y_sample)
```
